```python
import math
import jax, jax.numpy as jnp
from jax import lax
import numpy as np

D_MODEL = 1024
BATCH = 8
SEQ = 4096
DEPTH = 1

CHUNK = 64
D_MIX = D_MODEL
D_SSM = D_MIX // 2
SSM_GROUP = 16
N_SSM_GROUPS = D_SSM // SSM_GROUP
SSM_STATE = 64
D_ATT = D_MIX - D_SSM
HEAD_DIM = 64
N_HEADS = D_ATT // HEAD_DIM
Q_BLOCK = 128
N_EXPERTS = 64
TOP_K = 8
N_EXPERT_GROUPS = 8
TOPK_GROUPS = 4
D_EXPERT = 256
D_SHARED = 256
ROUTED_SCALE = 2.5
TOKEN_BLOCK = 128
EPS = 1e-6
DT_MIN = 1e-3
DT_MAX = 1e-1
LAMBDA_RE_MAX = -1e-4

kernel_name = "hymba_s5_stickbreak_moe_block"


def rms_norm(x, g):
    xf = x.astype(jnp.float32)
    y = xf * lax.rsqrt(jnp.mean(xf * xf, axis=-1, keepdims=True) + EPS)
    return (y * g.astype(jnp.float32)).astype(x.dtype)


def s5_mixer(u, lam_re, lam_im, b_re, b_im, c_re, c_im, d_skip, log_step, w_glu, b_glu):
    f32 = jnp.float32
    bsz, seq, _ = u.shape
    uf = u.astype(f32)
    lam = lax.complex(jnp.minimum(lam_re.astype(f32), LAMBDA_RE_MAX), lam_im.astype(f32))
    step = jnp.exp(log_step.astype(f32))[:, None]
    lam_bar = jnp.exp(lam * step)
    b_mat = lax.complex(b_re.astype(f32), b_im.astype(f32))
    b_bar = ((lam_bar - 1.0) / lam)[..., None] * b_mat
    c_mat = lax.complex(c_re.astype(f32), c_im.astype(f32))

    n_chunks = seq // CHUNK
    u_c = uf.reshape(bsz, n_chunks, CHUNK, N_SSM_GROUPS, SSM_GROUP).transpose(1, 2, 0, 3, 4)
    a_chunk = jnp.broadcast_to(lam_bar, (CHUNK, bsz, N_SSM_GROUPS, SSM_STATE))

    def combine(left, right):
        a_l, b_l = left
        a_r, b_r = right
        return a_r * a_l, a_r * b_l + b_r

    def chunk_step(h0, u_blk):
        bu = jnp.einsum('cbgh,gph->cbgp', u_blk.astype(jnp.complex64), b_bar)
        bu = bu.at[0].add(lam_bar * h0)
        _, states = lax.associative_scan(combine, (a_chunk, bu), axis=0)
        y = jnp.einsum('cbgp,ghp->cbgh', states, c_mat).real
        return states[-1], y

    h0 = jnp.zeros((bsz, N_SSM_GROUPS, SSM_STATE), jnp.complex64)
    _, y = lax.scan(chunk_step, h0, u_c)
    y = y.transpose(2, 0, 1, 3, 4).reshape(bsz, seq, D_SSM)
    y = jax.nn.gelu(y + d_skip.astype(f32) * uf)
    y = y * jax.nn.sigmoid(y @ w_glu.astype(f32) + b_glu.astype(f32))
    return y.astype(u.dtype)


def stick_breaking_attention(q, k, v):
    f32 = jnp.float32
    bsz, seq, _, _ = q.shape
    qh = q.astype(f32).transpose(0, 2, 1, 3)
    kh = k.astype(f32).transpose(0, 2, 1, 3)
    vh = v.astype(f32).transpose(0, 2, 1, 3)
    scale = 1.0 / math.sqrt(HEAD_DIM)
    outs = []
    for blk in range(seq // Q_BLOCK):
        start = blk * Q_BLOCK
        end = start + Q_BLOCK
        qb = qh[:, :, start:end]
        kp = kh[:, :, :end]
        vp = vh[:, :, :end]
        z = jnp.einsum('bhqd,bhkd->bhqk', qb, kp) * scale
        t_pos = start + jnp.arange(Q_BLOCK)[:, None]
        s_pos = jnp.arange(end)[None, :]
        mask = s_pos < t_pos
        log_1m = jnp.where(mask, jax.nn.log_sigmoid(-z), 0.0)
        after = lax.cumsum(log_1m, axis=3, reverse=True) - log_1m
        w = jnp.where(mask, jnp.exp(jax.nn.log_sigmoid(z) + after), 0.0)
        outs.append(jnp.einsum('bhqk,bhkd->bhqd', w, vp))
    o = jnp.concatenate(outs, axis=2)
    return o.transpose(0, 2, 1, 3).reshape(bsz, seq, D_ATT)


def swiglu(h, w1, w3, w2):
    return (jax.nn.silu(h @ w1) * (h @ w3)) @ w2


def moe_ffn(h, w_router, router_bias, e_w1, e_w3, e_w2, s_w1, s_w3, s_w2):
    f32 = jnp.float32
    bsz, seq, d = h.shape
    ht = h.reshape(-1, d)
    n_tok = ht.shape[0]
    scores = jax.nn.sigmoid((ht @ w_router).astype(f32))
    biased = scores + router_bias.astype(f32)
    grp = biased.reshape(n_tok, N_EXPERT_GROUPS, N_EXPERTS // N_EXPERT_GROUPS)
    grp_score = jnp.sum(lax.top_k(grp, 2)[0], axis=-1)
    _, grp_idx = lax.top_k(grp_score, TOPK_GROUPS)
    grp_mask = jnp.sum(jax.nn.one_hot(grp_idx, N_EXPERT_GROUPS, dtype=f32), axis=1)
    expert_mask = jnp.repeat(grp_mask, N_EXPERTS // N_EXPERT_GROUPS, axis=1)
    sel = jnp.where(expert_mask > 0, biased, -jnp.inf)
    _, idx = lax.top_k(sel, TOP_K)
    w = jnp.take_along_axis(scores, idx, axis=1)
    w = w / jnp.sum(w, axis=-1, keepdims=True) * ROUTED_SCALE
    gate = jnp.sum(jax.nn.one_hot(idx, N_EXPERTS, dtype=f32) * w[..., None], axis=1)

    def expert_block(args):
        xb, gb = args
        a = jnp.einsum('td,edf->tef', xb, e_w1)
        b = jnp.einsum('td,edf->tef', xb, e_w3)
        act = jax.nn.silu(a) * b * gb.astype(xb.dtype)[..., None]
        return jnp.einsum('tef,efd->td', act, e_w2)

    routed = lax.map(expert_block, (ht.reshape(-1, TOKEN_BLOCK, d),
                                    gate.reshape(-1, TOKEN_BLOCK, N_EXPERTS)))
    routed = routed.reshape(n_tok, d)
    shared = swiglu(ht, s_w1, s_w3, s_w2)
    return (routed + shared).reshape(bsz, seq, d)


def setup_inputs(seed: int = 0) -> dict:
    key = jax.random.key(seed)
    ks = jax.random.split(key, 32)
    f32 = jnp.float32
    L, G, P, H = DEPTH, N_SSM_GROUPS, SSM_STATE, SSM_GROUP
    nrm = lambda k, shape, s: jax.random.normal(k, shape, f32) * s
    gain = lambda k, shape: 1.0 + 0.01 * jax.random.normal(k, shape, f32)
    d_in = D_SSM + 3 * D_ATT
    return {
        "x": jax.random.normal(ks[0], (BATCH, SEQ, D_MODEL), f32),
        "g_mix": gain(ks[1], (L, D_MODEL)),
        "w_in": nrm(ks[2], (L, D_MODEL, d_in), D_MODEL ** -0.5),
        "lam_re": -0.5 + 0.01 * jax.random.normal(ks[3], (L, G, P), f32),
        "lam_im": jnp.pi * jnp.arange(P, dtype=f32)[None, None, :]
                  + 0.01 * jax.random.normal(ks[4], (L, G, P), f32),
        "b_re": nrm(ks[5], (L, G, P, H), (2.0 * H) ** -0.5),
        "b_im": nrm(ks[6], (L, G, P, H), (2.0 * H) ** -0.5),
        "c_re": nrm(ks[7], (L, G, H, P), (2.0 * P) ** -0.5),
        "c_im": nrm(ks[8], (L, G, H, P), (2.0 * P) ** -0.5),
        "d_skip": jax.random.normal(ks[9], (L, D_SSM), f32),
        "log_step": jax.random.uniform(ks[10], (L, G), f32, math.log(DT_MIN), math.log(DT_MAX)),
        "w_glu": nrm(ks[11], (L, D_SSM, D_SSM), D_SSM ** -0.5),
        "b_glu": nrm(ks[12], (L, D_SSM), 0.01),
        "g_q": gain(ks[13], (L, HEAD_DIM)),
        "g_k": gain(ks[14], (L, HEAD_DIM)),
        "g_out_ssm": gain(ks[15], (L, D_SSM)),
        "g_out_att": gain(ks[16], (L, D_ATT)),
        "w_out": nrm(ks[17], (L, D_MIX, D_MODEL), D_MIX ** -0.5),
        "g_ffn": gain(ks[18], (L, D_MODEL)),
        "w_router": nrm(ks[19], (L, D_MODEL, N_EXPERTS), D_MODEL ** -0.5),
        "router_bias": nrm(ks[20], (L, N_EXPERTS), 0.01),
        "e_w1": nrm(ks[21], (L, N_EXPERTS, D_MODEL, D_EXPERT), D_MODEL ** -0.5),
        "e_w3": nrm(ks[22], (L, N_EXPERTS, D_MODEL, D_EXPERT), D_MODEL ** -0.5),
        "e_w2": nrm(ks[23], (L, N_EXPERTS, D_EXPERT, D_MODEL), D_EXPERT ** -0.5),
        "s_w1": nrm(ks[24], (L, D_MODEL, D_SHARED), D_MODEL ** -0.5),
        "s_w3": nrm(ks[25], (L, D_MODEL, D_SHARED), D_MODEL ** -0.5),
        "s_w2": nrm(ks[26], (L, D_SHARED, D_MODEL), D_SHARED ** -0.5),
    }


def reference(x, g_mix, w_in, lam_re, lam_im, b_re, b_im, c_re, c_im, d_skip, log_step,
              w_glu, b_glu, g_q, g_k, g_out_ssm, g_out_att, w_out, g_ffn, w_router,
              router_bias, e_w1, e_w3, e_w2, s_w1, s_w3, s_w2):
    bsz, seq, _ = x.shape
    h = x
    for l in range(DEPTH):
        hn = rms_norm(h, g_mix[l])
        proj = hn @ w_in[l]
        u, q, k, v = jnp.split(proj, [D_SSM, D_SSM + D_ATT, D_SSM + 2 * D_ATT], axis=-1)
        y_ssm = s5_mixer(u, lam_re[l], lam_im[l], b_re[l], b_im[l], c_re[l], c_im[l],
                         d_skip[l], log_step[l], w_glu[l], b_glu[l])
        q = rms_norm(q.reshape(bsz, seq, N_HEADS, HEAD_DIM), g_q[l])
        k = rms_norm(k.reshape(bsz, seq, N_HEADS, HEAD_DIM), g_k[l])
        v = v.reshape(bsz, seq, N_HEADS, HEAD_DIM)
        y_att = stick_breaking_attention(q, k, v).astype(h.dtype)
        mixed = jnp.concatenate([rms_norm(y_ssm, g_out_ssm[l]),
                                 rms_norm(y_att, g_out_att[l])], axis=-1)
        h = h + mixed @ w_out[l]
        h = h + moe_ffn(rms_norm(h, g_ffn[l]), w_router[l], router_bias[l], e_w1[l],
                        e_w3[l], e_w2[l], s_w1[l], s_w3[l], s_w2[l])
    return h
```

```python
import functools
import math

import jax
import jax.numpy as jnp
from jax import lax
from jax.experimental import pallas as pl
from jax.experimental.pallas import tpu as pltpu

F32 = jnp.float32
BF16 = jnp.bfloat16

D_MODEL = 1024
D_SSM = 512
SSM_GROUP = 16
N_SSM_GROUPS = 32
SSM_STATE = 64
D_ATT = 512
HEAD_DIM = 64
N_EXPERTS = 64
TOP_K = 8
N_EXPERT_GROUPS = 8
TOPK_GROUPS = 4
D_EXPERT = 256
ROUTED_SCALE = 2.5
EPS = 1e-6
LAMBDA_RE_MAX = -1e-4

LANES = 128
SUBLANES = 8
VMEM_LIMIT = 56 * 1024 * 1024

N_STATE = N_SSM_GROUPS * SSM_STATE
GROUPS_PER_TILE = LANES // SSM_GROUP
N_LANE_TILES = D_SSM // LANES
STATE_PER_TILE = GROUPS_PER_TILE * SSM_STATE

ROW_TILE = 512
S5_CHUNK = 64
S5_COLS = 1024
ATT_TILE = 256
MOE_TILE = 1024

_NT = (((1,), (1,)), ((), ()))


def _rms(x, g):
    ms = jnp.mean(x * x, axis=-1, keepdims=True)
    return x * lax.rsqrt(ms + EPS) * g


def _dot(a, b):
    return jnp.dot(a, b, preferred_element_type=F32)


def _in_proj_kernel(x_ref, g_ref, w_ref, gq_ref, gk_ref, hsum_ref,
                    u_ref, q_ref, k_ref, v_ref):
    hn = _rms(x_ref[...], g_ref[...]).astype(BF16)
    u_ref[...] = _dot(hn, w_ref[:, 0:D_SSM]).astype(BF16)
    v_ref[...] = _dot(hn, w_ref[:, D_SSM + 2 * D_ATT:]).astype(BF16)

    def head_norm(lo, g):
        y = _dot(hn, w_ref[:, lo:lo + D_ATT])
        ms = _dot((y * y).astype(BF16), hsum_ref[...])
        return (y * lax.rsqrt(ms + EPS) * g).astype(BF16)

    q_ref[...] = head_norm(D_SSM, gq_ref[...])
    k_ref[...] = head_norm(D_SSM + D_ATT, gk_ref[...])


def _in_proj(x2, g_mix, w_in, gq, gk, hsum):
    n = x2.shape[0]
    row = lambda i: (i, 0)
    fixed = lambda i: (0, 0)
    out = jax.ShapeDtypeStruct((n, D_SSM), BF16)
    return pl.pallas_call(
        _in_proj_kernel,
        grid=(n // ROW_TILE,),
        in_specs=[
            pl.BlockSpec((ROW_TILE, D_MODEL), row),
            pl.BlockSpec((1, D_MODEL), fixed),
            pl.BlockSpec(w_in.shape, fixed),
            pl.BlockSpec((1, D_ATT), fixed),
            pl.BlockSpec((1, D_ATT), fixed),
            pl.BlockSpec((D_ATT, D_ATT), fixed),
        ],
        out_specs=[pl.BlockSpec((ROW_TILE, D_SSM), row)] * 4,
        out_shape=[out] * 4,
        compiler_params=pltpu.CompilerParams(
            dimension_semantics=("parallel",), vmem_limit_bytes=VMEM_LIMIT),
        name="in_proj",
    )(x2, g_mix, w_in, gq, gk, hsum)


def _gelu_tanh(x):
    c = math.sqrt(2.0 / math.pi)
    return 0.5 * x * (1.0 + jnp.tanh(c * (x + 0.044715 * (x * x * x))))


def _s5_kernel(u_ref, bre_ref, bim_ref, lr_ref, li_ref, cre_ref, cimn_ref,
               d_ref, wg_ref, bg_ref, go_ref, o_ref, sre, sim, hre, him):
    @pl.when(pl.program_id(0) == 0)
    def _():
        hre[...] = jnp.zeros_like(hre)
        him[...] = jnp.zeros_like(him)

    u = u_ref[...]
    for a in range(N_LANE_TILES):
        ua = u[:, a * LANES:(a + 1) * LANES]
        cols = slice(a * STATE_PER_TILE, (a + 1) * STATE_PER_TILE)
        sre[:, cols] = _dot(ua, bre_ref[a])
        sim[:, cols] = _dot(ua, bim_ref[a])

    for cb in range(N_STATE // S5_COLS):
        cols = slice(cb * S5_COLS, (cb + 1) * S5_COLS)
        lr = jnp.broadcast_to(lr_ref[:, cols], (SUBLANES, S5_COLS))
        li = jnp.broadcast_to(li_ref[:, cols], (SUBLANES, S5_COLS))

        def step(t, h, cols=cols, lr=lr, li=li):
            hr, hi = h
            rows = pl.ds(pl.multiple_of(t * SUBLANES, SUBLANES), SUBLANES)
            nr = lr * hr - li * hi + sre[rows, cols]
            ni = lr * hi + li * hr + sim[rows, cols]
            sre[rows, cols] = nr
            sim[rows, cols] = ni
            return nr, ni

        hr, hi = lax.fori_loop(0, S5_CHUNK, step, (hre[:, cols], him[:, cols]),
                               unroll=2)
        hre[:, cols] = hr
        him[:, cols] = hi

    ys = []
    for a in range(N_LANE_TILES):
        cols = slice(a * STATE_PER_TILE, (a + 1) * STATE_PER_TILE)
        ys.append(_dot(sre[:, cols].astype(BF16), cre_ref[a])
                  + _dot(sim[:, cols].astype(BF16), cimn_ref[a]))
    y = jnp.concatenate(ys, axis=1)
    y = _gelu_tanh(y + d_ref[...] * u.astype(F32))
    y = y * jax.nn.sigmoid(_dot(y.astype(BF16), wg_ref[...]) + bg_ref[...])
    o_ref[...] = _rms(y, go_ref[...]).astype(BF16)


def _s5(u_tb, bre, bim, lr, li, cre, cimn, d_skip, w_glu, b_glu, g_out):
    n = u_tb.shape[0]
    rows = S5_CHUNK * SUBLANES
    fixed2 = lambda i: (0, 0)
    fixed3 = lambda i: (0, 0, 0)
    return pl.pallas_call(
        _s5_kernel,
        grid=(n // rows,),
        in_specs=[
            pl.BlockSpec((rows, D_SSM), lambda i: (i, 0)),
            pl.BlockSpec(bre.shape, fixed3),
            pl.BlockSpec(bim.shape, fixed3),
            pl.BlockSpec((1, N_STATE), fixed2),
            pl.BlockSpec((1, N_STATE), fixed2),
            pl.BlockSpec(cre.shape, fixed3),
            pl.BlockSpec(cimn.shape, fixed3),
            pl.BlockSpec((1, D_SSM), fixed2),
            pl.BlockSpec((D_SSM, D_SSM), fixed2),
            pl.BlockSpec((1, D_SSM), fixed2),
            pl.BlockSpec((1, D_SSM), fixed2),
        ],
        out_specs=pl.BlockSpec((rows, D_SSM), lambda i: (i, 0)),
        out_shape=jax.ShapeDtypeStruct((n, D_SSM), BF16),
        scratch_shapes=[
            pltpu.VMEM((rows, N_STATE), F32),
            pltpu.VMEM((rows, N_STATE), F32),
            pltpu.VMEM((SUBLANES, N_STATE), F32),
            pltpu.VMEM((SUBLANES, N_STATE), F32),
        ],
        compiler_params=pltpu.CompilerParams(
            dimension_semantics=("arbitrary",), vmem_limit_bytes=VMEM_LIMIT),
        name="s5_mixer",
    )(u_tb, bre, bim, lr, li, cre, cimn, d_skip, w_glu, b_glu, g_out)


def _s5_weights(lam_re, lam_im, b_re, b_im, c_re, c_im, log_step):
    lre = jnp.minimum(lam_re, LAMBDA_RE_MAX)
    step = jnp.exp(log_step)[:, None]
    mag = jnp.exp(lre * step)
    bar_re = mag * jnp.cos(lam_im * step)
    bar_im = mag * jnp.sin(lam_im * step)
    den = lre * lre + lam_im * lam_im
    coef_re = ((bar_re - 1.0) * lre + bar_im * lam_im) / den
    coef_im = (bar_im * lre - (bar_re - 1.0) * lam_im) / den
    bbar_re = coef_re[..., None] * b_re - coef_im[..., None] * b_im
    bbar_im = coef_re[..., None] * b_im + coef_im[..., None] * b_re
    eye = jnp.eye(GROUPS_PER_TILE, dtype=F32)

    def pack_b(b):
        b = b.reshape(N_LANE_TILES, GROUPS_PER_TILE, SSM_STATE, SSM_GROUP)
        m = jnp.einsum('agph,gk->aghkp', b, eye)
        return m.reshape(N_LANE_TILES, LANES, STATE_PER_TILE).astype(BF16)

    def pack_c(c):
        c = c.reshape(N_LANE_TILES, GROUPS_PER_TILE, SSM_GROUP, SSM_STATE)
        m = jnp.einsum('aghp,gk->agpkh', c, eye)
        return m.reshape(N_LANE_TILES, STATE_PER_TILE, LANES).astype(BF16)

    return (pack_b(bbar_re), pack_b(bbar_im),
            bar_re.reshape(1, N_STATE), bar_im.reshape(1, N_STATE),
            pack_c(c_re), pack_c(-c_im))


def _attn_kernel(q_ref, k_ref, v_ref, tri_ref, o_ref, acc_ref, car_ref):
    t = ATT_TILE
    i = pl.program_id(2)
    lane = lax.broadcasted_iota(jnp.int32, (1, LANES), 1)
    strictly_before = (lax.broadcasted_iota(jnp.int32, (t, t), 1)
                       < lax.broadcasted_iota(jnp.int32, (t, t), 0))
    q = q_ref[0]
    out = jnp.zeros((t, LANES), F32)

    for h in range(LANES // HEAD_DIM):
        in_head = (lane >= h * HEAD_DIM) & (lane < (h + 1) * HEAD_DIM)
        qh = jnp.where(in_head, q, jnp.zeros_like(q))

        def block(j, on_diagonal, qh=qh):
            start = pl.multiple_of(j * t, t)
            kb = k_ref[0, pl.ds(start, t), :]
            vb = v_ref[0, pl.ds(start, t), :]
            z = lax.dot_general(qh, kb, _NT, preferred_element_type=F32)
            sp = jnp.maximum(z, 0.0) + jnp.log(1.0 + jnp.exp(-jnp.abs(z)))
            if on_diagonal:
                sp = jnp.where(strictly_before, sp, 0.0)
            sums = _dot(sp.astype(BF16), tri_ref[...])
            car = car_ref[...]
            later = sums[:, :t] + jnp.concatenate([car] * (t // LANES), axis=1)
            w = jnp.exp(z - sp - later)
            if on_diagonal:
                w = jnp.where(strictly_before, w, 0.0)
            acc_ref[...] += _dot(w.astype(BF16), vb)
            car_ref[...] = car + sums[:, t:]

        acc_ref[...] = jnp.zeros_like(acc_ref)
        car_ref[...] = jnp.zeros_like(car_ref)
        block(i, True)

        def earlier(jj, c):
            block(i - 1 - jj, False)
            return c

        lax.fori_loop(0, i, earlier, 0)
        out = jnp.where(in_head, acc_ref[...], out)

    o_ref[0] = out.astype(o_ref.dtype)


def _attention(q, k, v, tri):
    bsz, seq, _ = q.shape
    t = ATT_TILE
    return pl.pallas_call(
        _attn_kernel,
        grid=(bsz, D_ATT // LANES, seq // t),
        in_specs=[
            pl.BlockSpec((1, t, LANES), lambda b, h, i: (b, i, h)),
            pl.BlockSpec((1, seq, LANES), lambda b, h, i: (b, 0, h)),
            pl.BlockSpec((1, seq, LANES), lambda b, h, i: (b, 0, h)),
            pl.BlockSpec(tri.shape, lambda b, h, i: (0, 0)),
        ],
        out_specs=pl.BlockSpec((1, t, LANES), lambda b, h, i: (b, i, h)),
        out_shape=jax.ShapeDtypeStruct((bsz, seq, D_ATT), BF16),
        scratch_shapes=[pltpu.VMEM((t, LANES), F32), pltpu.VMEM((t, LANES), F32)],
        compiler_params=pltpu.CompilerParams(
            dimension_semantics=("parallel", "parallel", "arbitrary"),
            vmem_limit_bytes=VMEM_LIMIT),
        name="stick_breaking_attention",
    )(q, k, v, tri)


def _pick_first_max(vals, taken, ridx, n):
    cand = jnp.where(taken > 0, -jnp.inf, vals)
    m = jnp.max(cand, axis=0, keepdims=True)
    first = jnp.min(jnp.where((cand == m) & (taken == 0), ridx, n), axis=0, keepdims=True)
    return jnp.where(ridx == first, 1.0, taken)


def _route(scores, biased):
    tm = scores.shape[1]
    per_group = N_EXPERTS // N_EXPERT_GROUPS
    ridx8 = lax.broadcasted_iota(jnp.int32, (per_group, tm), 0)
    zeros8 = jnp.zeros((per_group, tm), F32)
    group_scores = []
    for g in range(N_EXPERT_GROUPS):
        blk = biased[g * per_group:(g + 1) * per_group]
        top1 = _pick_first_max(blk, zeros8, ridx8, per_group)
        top2 = _pick_first_max(blk, top1, ridx8, per_group)
        group_scores.append(jnp.sum(jnp.where(top2 > 0, blk, 0.0), axis=0, keepdims=True))
    grp = jnp.concatenate(group_scores, axis=0)
    ridx_g = lax.broadcasted_iota(jnp.int32, (N_EXPERT_GROUPS, tm), 0)
    grp_taken = jnp.zeros((N_EXPERT_GROUPS, tm), F32)
    for _ in range(TOPK_GROUPS):
        grp_taken = _pick_first_max(grp, grp_taken, ridx_g, N_EXPERT_GROUPS)
    expert_mask = jnp.concatenate(
        [jnp.broadcast_to(grp_taken[g:g + 1], (per_group, tm)) for g in range(N_EXPERT_GROUPS)],
        axis=0)
    sel = jnp.where(expert_mask > 0, biased, -jnp.inf)
    ridx = lax.broadcasted_iota(jnp.int32, (N_EXPERTS, tm), 0)
    taken = jnp.zeros((N_EXPERTS, tm), F32)
    for _ in range(TOP_K):
        taken = _pick_first_max(sel, taken, ridx, N_EXPERTS)
    w = jnp.where(taken > 0, scores, 0.0)
    return w / jnp.sum(w, axis=0, keepdims=True) * ROUTED_SCALE


def _post_mix_kernel(x_ref, ssm_ref, att_ref, ga_ref, wos_ref, woa_ref, gf_ref,
                     wrh_ref, wrl_ref, rb_ref, s1_ref, s3_ref, s2_ref,
                     base_ref, hn_ref, gate_ref):
    att_n = _rms(att_ref[...].astype(F32), ga_ref[...]).astype(BF16)
    h = x_ref[...] + _dot(ssm_ref[...], wos_ref[...]) + _dot(att_n, woa_ref[...])
    hn = _rms(h, gf_ref[...])
    hn_hi = hn.astype(BF16)
    hn_lo = (hn - hn_hi.astype(F32)).astype(BF16)
    hn_ref[...] = hn_hi

    nt = functools.partial(lax.dot_general, dimension_numbers=_NT, preferred_element_type=F32)
    logits = (nt(wrh_ref[...], hn_hi) + nt(wrh_ref[...], hn_lo) + nt(wrl_ref[...], hn_hi))
    scores = jax.nn.sigmoid(logits)
    gate_t = _route(scores, scores + rb_ref[...])
    pad = jnp.zeros((LANES - N_EXPERTS, gate_t.shape[1]), F32)
    gate_ref[...] = jnp.concatenate([gate_t, pad], axis=0).T

    a = _dot(hn_hi, s1_ref[...])
    b = _dot(hn_hi, s3_ref[...])
    act = (a * jax.nn.sigmoid(a) * b).astype(BF16)
    base_ref[...] = h + _dot(act, s2_ref[...])


def _post_mix(x2, ssm, att, g_att, wo_s, wo_a, g_ffn, wr_hi, wr_lo, rbias, s1, s3, s2):
    n = x2.shape[0]
    row = lambda i: (i, 0)
    fixed = lambda i: (0, 0)
    full = lambda a: pl.BlockSpec(a.shape, fixed)
    return pl.pallas_call(
        _post_mix_kernel,
        grid=(n // ROW_TILE,),
        in_specs=[
            pl.BlockSpec((ROW_TILE, D_MODEL), row),
            pl.BlockSpec((ROW_TILE, D_SSM), row),
            pl.BlockSpec((ROW_TILE, D_ATT), row),
            full(g_att), full(wo_s), full(wo_a), full(g_ffn),
            full(wr_hi), full(wr_lo), full(rbias), full(s1), full(s3), full(s2),
        ],
        out_specs=[
            pl.BlockSpec((ROW_TILE, D_MODEL), row),
            pl.BlockSpec((ROW_TILE, D_MODEL), row),
            pl.BlockSpec((ROW_TILE, LANES), row),
        ],
        out_shape=[
            jax.ShapeDtypeStruct((n, D_MODEL), F32),
            jax.ShapeDtypeStruct((n, D_MODEL), BF16),
            jax.ShapeDtypeStruct((n, LANES), F32),
        ],
        compiler_params=pltpu.CompilerParams(
            dimension_semantics=("parallel",), vmem_limit_bytes=VMEM_LIMIT),
        name="post_mix",
    )(x2, ssm, att, g_att, wo_s, wo_a, g_ffn, wr_hi, wr_lo, rbias, s1, s3, s2)


def _moe_kernel(hn_ref, gate_ref, base_ref, w1_ref, w3_ref, w2_ref, o_ref):
    e = pl.program_id(1)

    @pl.when(e == 0)
    def _():
        o_ref[...] = base_ref[...]

    hn = hn_ref[...]
    a = _dot(hn, w1_ref[0])
    b = _dot(hn, w3_ref[0])
    lane = lax.broadcasted_iota(jnp.int32, (1, LANES), 1)
    g = jnp.sum(jnp.where(lane == e, gate_ref[...], 0.0), axis=-1, keepdims=True)
    act = (a * jax.nn.sigmoid(a) * b * g).astype(BF16)
    o_ref[...] += _dot(act, w2_ref[0])


def _moe(hn, gate, base, w1, w3, w2):
    n = hn.shape[0]
    row = lambda i, e: (i, 0)
    return pl.pallas_call(
        _moe_kernel,
        grid=(n // MOE_TILE, N_EXPERTS),
        in_specs=[
            pl.BlockSpec((MOE_TILE, D_MODEL), row),
            pl.BlockSpec((MOE_TILE, LANES), row),
            pl.BlockSpec((MOE_TILE, D_MODEL), row),
            pl.BlockSpec((1, D_MODEL, D_EXPERT), lambda i, e: (e, 0, 0)),
            pl.BlockSpec((1, D_MODEL, D_EXPERT), lambda i, e: (e, 0, 0)),
            pl.BlockSpec((1, D_EXPERT, D_MODEL), lambda i, e: (e, 0, 0)),
        ],
        out_specs=pl.BlockSpec((MOE_TILE, D_MODEL), row),
        out_shape=jax.ShapeDtypeStruct((n, D_MODEL), F32),
        compiler_params=pltpu.CompilerParams(
            dimension_semantics=("parallel", "arbitrary"), vmem_limit_bytes=VMEM_LIMIT),
        name="moe_experts",
    )(hn, gate, base, w1, w3, w2)


def _layer(h, g_mix, w_in, lam_re, lam_im, b_re, b_im, c_re, c_im, d_skip, log_step,
           w_glu, b_glu, g_q, g_k, g_out_ssm, g_out_att, w_out, g_ffn, w_router,
           router_bias, e_w1, e_w3, e_w2, s_w1, s_w3, s_w2):
    bsz, seq, d = h.shape
    n = bsz * seq
    x2 = h.reshape(n, d)
    n_heads = D_ATT // HEAD_DIM

    head_id = jnp.arange(D_ATT) // HEAD_DIM
    hsum = jnp.where(head_id[:, None] == head_id[None, :], 1.0 / HEAD_DIM, 0.0).astype(BF16)
    ids = jnp.arange(ATT_TILE)
    tri = jnp.concatenate([(ids[:, None] > ids[None, :]).astype(BF16),
                           jnp.ones((ATT_TILE, LANES), BF16)], axis=1)

    gq = (jnp.tile(g_q, n_heads) * (1.0 / math.sqrt(HEAD_DIM))).reshape(1, D_ATT)
    gk = jnp.tile(g_k, n_heads).reshape(1, D_ATT)
    u, q, k, v = _in_proj(x2, g_mix.reshape(1, d), w_in.astype(BF16), gq, gk, hsum)

    u_tb = u.reshape(bsz, seq, D_SSM).transpose(1, 0, 2).reshape(n, D_SSM)
    s5w = _s5_weights(lam_re, lam_im, b_re, b_im, c_re, c_im, log_step)
    ssm_tb = _s5(u_tb, *s5w, d_skip.reshape(1, D_SSM), w_glu.astype(BF16),
                 b_glu.reshape(1, D_SSM), g_out_ssm.reshape(1, D_SSM))
    ssm = ssm_tb.reshape(seq, bsz, D_SSM).transpose(1, 0, 2).reshape(n, D_SSM)

    att = _attention(q.reshape(bsz, seq, D_ATT), k.reshape(bsz, seq, D_ATT),
                     v.reshape(bsz, seq, D_ATT), tri).reshape(n, D_ATT)

    wr_t = w_router.T
    wr_hi = wr_t.astype(BF16)
    wr_lo = (wr_t - wr_hi.astype(F32)).astype(BF16)
    w_out_b = w_out.astype(BF16)
    base, hn, gate = _post_mix(
        x2, ssm, att, g_out_att.reshape(1, D_ATT), w_out_b[:D_SSM], w_out_b[D_SSM:],
        g_ffn.reshape(1, d), wr_hi, wr_lo, router_bias.reshape(N_EXPERTS, 1),
        s_w1.astype(BF16), s_w3.astype(BF16), s_w2.astype(BF16))

    out = _moe(hn, gate, base, e_w1.astype(BF16), e_w3.astype(BF16), e_w2.astype(BF16))
    return out.reshape(bsz, seq, d)


def kernel(x, g_mix, w_in, lam_re, lam_im, b_re, b_im, c_re, c_im, d_skip, log_step,
           w_glu, b_glu, g_q, g_k, g_out_ssm, g_out_att, w_out, g_ffn, w_router,
           router_bias, e_w1, e_w3, e_w2, s_w1, s_w3, s_w2):
    h = x
    for l in range(g_mix.shape[0]):
        h = _layer(h, g_mix[l], w_in[l], lam_re[l], lam_im[l], b_re[l], b_im[l], c_re[l],
                   c_im[l], d_skip[l], log_step[l], w_glu[l], b_glu[l], g_q[l], g_k[l],
                   g_out_ssm[l], g_out_att[l], w_out[l], g_ffn[l], w_router[l],
                   router_bias[l], e_w1[l], e_w3[l], e_w2[l], s_w1[l], s_w3[l], s_w2[l])
    return h
```

```python
import functools
import math

import jax
import jax.numpy as jnp
from jax import lax
from jax.experimental import pallas as pl
from jax.experimental.pallas import tpu as pltpu

F32 = jnp.float32
BF16 = jnp.bfloat16

D_MODEL = 1024
D_SSM = 512
SSM_GROUP = 16
N_SSM_GROUPS = 32
SSM_STATE = 64
D_ATT = 512
HEAD_DIM = 64
N_EXPERTS = 64
TOP_K = 8
N_EXPERT_GROUPS = 8
TOPK_GROUPS = 4
D_EXPERT = 256
ROUTED_SCALE = 2.5
EPS = 1e-6
LAMBDA_RE_MAX = -1e-4

LANES = 128
SUBLANES = 8
VMEM_LIMIT = 56 * 1024 * 1024

N_STATE = N_SSM_GROUPS * SSM_STATE
GROUPS_PER_TILE = LANES // SSM_GROUP
N_LANE_TILES = D_SSM // LANES
STATE_PER_TILE = GROUPS_PER_TILE * SSM_STATE

ROW_TILE = 512
S5_CHUNK = 64
S5_COLS = 1024
ATT_TILE = LANES
EXP2_UNDERFLOW = 174.0
MOE_BLOCK = 256
CHUNK = 16
BLOCK_CHUNKS = MOE_BLOCK * TOP_K // CHUNK + N_EXPERTS
BLOCK_SLOTS = BLOCK_CHUNKS * CHUNK
EXPERT_ROWS = 512

_NT = (((1,), (1,)), ((), ()))


def _rms(x, g):
    ms = jnp.mean(x * x, axis=-1, keepdims=True)
    return x * lax.rsqrt(ms + EPS) * g


def _dot(a, b):
    return jnp.dot(a, b, preferred_element_type=F32)


def _in_proj_kernel(x_ref, g_ref, w_ref, gq_ref, gk_ref, hsum_ref,
                    u_ref, q_ref, k_ref, v_ref):
    hn = _rms(x_ref[...], g_ref[...]).astype(BF16)
    u_ref[...] = _dot(hn, w_ref[:, 0:D_SSM]).astype(BF16)
    v_ref[...] = _dot(hn, w_ref[:, D_SSM + 2 * D_ATT:]).astype(BF16)

    def head_norm(lo, g):
        y = _dot(hn, w_ref[:, lo:lo + D_ATT])
        ms = _dot((y * y).astype(BF16), hsum_ref[...])
        return (y * lax.rsqrt(ms + EPS) * g).astype(BF16)

    q_ref[...] = head_norm(D_SSM, gq_ref[...])
    k_ref[...] = head_norm(D_SSM + D_ATT, gk_ref[...])


def _in_proj(x2, g_mix, w_in, gq, gk, hsum):
    n = x2.shape[0]
    row = lambda i: (i, 0)
    fixed = lambda i: (0, 0)
    out = jax.ShapeDtypeStruct((n, D_SSM), BF16)
    return pl.pallas_call(
        _in_proj_kernel,
        grid=(n // ROW_TILE,),
        in_specs=[
            pl.BlockSpec((ROW_TILE, D_MODEL), row),
            pl.BlockSpec((1, D_MODEL), fixed),
            pl.BlockSpec(w_in.shape, fixed),
            pl.BlockSpec((1, D_ATT), fixed),
            pl.BlockSpec((1, D_ATT), fixed),
            pl.BlockSpec((D_ATT, D_ATT), fixed),
        ],
        out_specs=[pl.BlockSpec((ROW_TILE, D_SSM), row)] * 4,
        out_shape=[out] * 4,
        compiler_params=pltpu.CompilerParams(
            dimension_semantics=("parallel",), vmem_limit_bytes=VMEM_LIMIT),
        name="in_proj",
    )(x2, g_mix, w_in, gq, gk, hsum)


def _gelu_tanh(x):
    c = math.sqrt(2.0 / math.pi)
    return 0.5 * x * (1.0 + jnp.tanh(c * (x + 0.044715 * (x * x * x))))


def _s5_kernel(u_ref, bre_ref, bim_ref, lr_ref, li_ref, cre_ref, cimn_ref,
               d_ref, wg_ref, bg_ref, go_ref, o_ref, sre, sim, hre, him):
    @pl.when(pl.program_id(0) == 0)
    def _():
        hre[...] = jnp.zeros_like(hre)
        him[...] = jnp.zeros_like(him)

    u = u_ref[...]
    for a in range(N_LANE_TILES):
        ua = u[:, a * LANES:(a + 1) * LANES]
        cols = slice(a * STATE_PER_TILE, (a + 1) * STATE_PER_TILE)
        sre[:, cols] = _dot(ua, bre_ref[a])
        sim[:, cols] = _dot(ua, bim_ref[a])

    for cb in range(N_STATE // S5_COLS):
        cols = slice(cb * S5_COLS, (cb + 1) * S5_COLS)
        lr = jnp.broadcast_to(lr_ref[:, cols], (SUBLANES, S5_COLS))
        li = jnp.broadcast_to(li_ref[:, cols], (SUBLANES, S5_COLS))

        def step(t, h, cols=cols, lr=lr, li=li):
            hr, hi = h
            rows = pl.ds(pl.multiple_of(t * SUBLANES, SUBLANES), SUBLANES)
            nr = lr * hr - li * hi + sre[rows, cols]
            ni = lr * hi + li * hr + sim[rows, cols]
            sre[rows, cols] = nr
            sim[rows, cols] = ni
            return nr, ni

        hr, hi = lax.fori_loop(0, S5_CHUNK, step, (hre[:, cols], him[:, cols]),
                               unroll=2)
        hre[:, cols] = hr
        him[:, cols] = hi

    ys = []
    for a in range(N_LANE_TILES):
        cols = slice(a * STATE_PER_TILE, (a + 1) * STATE_PER_TILE)
        ys.append(_dot(sre[:, cols].astype(BF16), cre_ref[a])
                  + _dot(sim[:, cols].astype(BF16), cimn_ref[a]))
    y = jnp.concatenate(ys, axis=1)
    y = _gelu_tanh(y + d_ref[...] * u.astype(F32))
    y = y * jax.nn.sigmoid(_dot(y.astype(BF16), wg_ref[...]) + bg_ref[...])
    o_ref[...] = _rms(y, go_ref[...]).astype(BF16)


def _s5(u_tb, bre, bim, lr, li, cre, cimn, d_skip, w_glu, b_glu, g_out):
    n = u_tb.shape[0]
    rows = S5_CHUNK * SUBLANES
    fixed2 = lambda i: (0, 0)
    fixed3 = lambda i: (0, 0, 0)
    return pl.pallas_call(
        _s5_kernel,
        grid=(n // rows,),
        in_specs=[
            pl.BlockSpec((rows, D_SSM), lambda i: (i, 0)),
            pl.BlockSpec(bre.shape, fixed3),
            pl.BlockSpec(bim.shape, fixed3),
            pl.BlockSpec((1, N_STATE), fixed2),
            pl.BlockSpec((1, N_STATE), fixed2),
            pl.BlockSpec(cre.shape, fixed3),
            pl.BlockSpec(cimn.shape, fixed3),
            pl.BlockSpec((1, D_SSM), fixed2),
            pl.BlockSpec((D_SSM, D_SSM), fixed2),
            pl.BlockSpec((1, D_SSM), fixed2),
            pl.BlockSpec((1, D_SSM), fixed2),
        ],
        out_specs=pl.BlockSpec((rows, D_SSM), lambda i: (i, 0)),
        out_shape=jax.ShapeDtypeStruct((n, D_SSM), BF16),
        scratch_shapes=[
            pltpu.VMEM((rows, N_STATE), F32),
            pltpu.VMEM((rows, N_STATE), F32),
            pltpu.VMEM((SUBLANES, N_STATE), F32),
            pltpu.VMEM((SUBLANES, N_STATE), F32),
        ],
        compiler_params=pltpu.CompilerParams(
            dimension_semantics=("arbitrary",), vmem_limit_bytes=VMEM_LIMIT),
        name="s5_mixer",
    )(u_tb, bre, bim, lr, li, cre, cimn, d_skip, w_glu, b_glu, g_out)


def _s5_weights(lam_re, lam_im, b_re, b_im, c_re, c_im, log_step):
    lre = jnp.minimum(lam_re, LAMBDA_RE_MAX)
    step = jnp.exp(log_step)[:, None]
    mag = jnp.exp(lre * step)
    bar_re = mag * jnp.cos(lam_im * step)
    bar_im = mag * jnp.sin(lam_im * step)
    den = lre * lre + lam_im * lam_im
    coef_re = ((bar_re - 1.0) * lre + bar_im * lam_im) / den
    coef_im = (bar_im * lre - (bar_re - 1.0) * lam_im) / den
    bbar_re = coef_re[..., None] * b_re - coef_im[..., None] * b_im
    bbar_im = coef_re[..., None] * b_im + coef_im[..., None] * b_re
    eye = jnp.eye(GROUPS_PER_TILE, dtype=F32)

    def pack_b(b):
        b = b.reshape(N_LANE_TILES, GROUPS_PER_TILE, SSM_STATE, SSM_GROUP)
        m = jnp.einsum('agph,gk->aghkp', b, eye)
        return m.reshape(N_LANE_TILES, LANES, STATE_PER_TILE).astype(BF16)

    def pack_c(c):
        c = c.reshape(N_LANE_TILES, GROUPS_PER_TILE, SSM_GROUP, SSM_STATE)
        m = jnp.einsum('aghp,gk->agpkh', c, eye)
        return m.reshape(N_LANE_TILES, STATE_PER_TILE, LANES).astype(BF16)

    return (pack_b(bbar_re), pack_b(bbar_im),
            bar_re.reshape(1, N_STATE), bar_im.reshape(1, N_STATE),
            pack_c(c_re), pack_c(-c_im))


def _attn_kernel(q_ref, k_ref, v_ref, tri_ref, o_ref, *scratch):
    acc_refs, car_refs = scratch[:D_ATT // LANES], scratch[D_ATT // LANES:]
    t = ATT_TILE
    i = pl.program_id(1)
    first_head = lax.broadcasted_iota(jnp.int32, (1, LANES), 1) < HEAD_DIM

    def split_heads(x):
        zero = jnp.zeros_like(x)
        return jnp.concatenate([jnp.where(first_head, x, zero),
                                jnp.where(first_head, zero, x)], axis=0)

    def block(j, on_diagonal):
        start = pl.multiple_of(j * t, t)
        if on_diagonal:
            col = lax.broadcasted_iota(jnp.int32, (t, 2 * t), 1)
            row = lax.broadcasted_iota(jnp.int32, (t, 2 * t), 0)
            strictly_before = jnp.where(col >= t, col - t, col) < row
        tiles = [slice(p * LANES, (p + 1) * LANES) for p in range(D_ATT // LANES)]
        zs = [lax.dot_general(q_ref[0, :, lanes], split_heads(k_ref[0, pl.ds(start, t), lanes]),
                              _NT, preferred_element_type=F32) for lanes in tiles]
        sps = [jnp.maximum(z, 0.0) + jnp.log2(1.0 + jnp.exp2(-jnp.abs(z))) for z in zs]
        if on_diagonal:
            sps = [jnp.where(strictly_before, sp, 0.0) for sp in sps]
        sums = [_dot(sp.astype(BF16), tri_ref[...]) for sp in sps]
        smallest = None
        for z, sp, sm, lanes, acc_ref, car_ref in zip(zs, sps, sums, tiles, acc_refs, car_refs):
            car = car_ref[...]
            w = jnp.exp2(z - sp - (sm[:, :2 * t] + car))
            if on_diagonal:
                w = jnp.where(strictly_before, w, 0.0)
            acc_ref[...] += _dot(w.astype(BF16), split_heads(v_ref[0, pl.ds(start, t), lanes]))
            car = car + sm[:, 2 * t:]
            car_ref[...] = car
            smallest = car if smallest is None else jnp.minimum(smallest, car)
        return jnp.min(smallest)

    for ref in scratch:
        ref[...] = jnp.zeros_like(ref)
    carried = block(i, True)

    def more(state):
        j, carried = state
        return (j >= 0) & (carried < EXP2_UNDERFLOW)

    def earlier(state):
        j, _ = state
        return j - 1, block(j, False)

    lax.while_loop(more, earlier, (i - 1, carried))
    o_ref[0] = jnp.concatenate(
        [ref[...] for ref in acc_refs], axis=1).astype(o_ref.dtype)


def _attention(q, k, v, tri):
    bsz, seq, _ = q.shape
    t = ATT_TILE
    n_tiles = D_ATT // LANES
    return pl.pallas_call(
        _attn_kernel,
        grid=(bsz, seq // t),
        in_specs=[
            pl.BlockSpec((1, t, D_ATT), lambda b, i: (b, i, 0)),
            pl.BlockSpec((1, seq, D_ATT), lambda b, i: (b, 0, 0)),
            pl.BlockSpec((1, seq, D_ATT), lambda b, i: (b, 0, 0)),
            pl.BlockSpec(tri.shape, lambda b, i: (0, 0)),
        ],
        out_specs=pl.BlockSpec((1, t, D_ATT), lambda b, i: (b, i, 0)),
        out_shape=jax.ShapeDtypeStruct((bsz, seq, D_ATT), BF16),
        scratch_shapes=([pltpu.VMEM((t, LANES), F32)] * n_tiles
                        + [pltpu.VMEM((t, 2 * t), F32)] * n_tiles),
        compiler_params=pltpu.CompilerParams(
            dimension_semantics=("parallel", "parallel"),
            vmem_limit_bytes=VMEM_LIMIT),
        name="stick_breaking_attention",
    )(q, k, v, tri)


def _pick_first_max(vals, taken, ridx, n):
    cand = jnp.where(taken > 0, -jnp.inf, vals)
    m = jnp.max(cand, axis=0, keepdims=True)
    first = jnp.min(jnp.where((cand == m) & (taken == 0), ridx, n), axis=0, keepdims=True)
    return jnp.where(ridx == first, 1.0, taken)


def _route(scores, biased):
    tm = scores.shape[1]
    per_group = N_EXPERTS // N_EXPERT_GROUPS
    ridx8 = lax.broadcasted_iota(jnp.int32, (per_group, tm), 0)
    zeros8 = jnp.zeros((per_group, tm), F32)
    group_scores = []
    for g in range(N_EXPERT_GROUPS):
        blk = biased[g * per_group:(g + 1) * per_group]
        top1 = _pick_first_max(blk, zeros8, ridx8, per_group)
        top2 = _pick_first_max(blk, top1, ridx8, per_group)
        group_scores.append(jnp.sum(jnp.where(top2 > 0, blk, 0.0), axis=0, keepdims=True))
    grp = jnp.concatenate(group_scores, axis=0)
    ridx_g = lax.broadcasted_iota(jnp.int32, (N_EXPERT_GROUPS, tm), 0)
    grp_taken = jnp.zeros((N_EXPERT_GROUPS, tm), F32)
    for _ in range(TOPK_GROUPS):
        grp_taken = _pick_first_max(grp, grp_taken, ridx_g, N_EXPERT_GROUPS)
    expert_mask = jnp.concatenate(
        [jnp.broadcast_to(grp_taken[g:g + 1], (per_group, tm)) for g in range(N_EXPERT_GROUPS)],
        axis=0)
    sel = jnp.where(expert_mask > 0, biased, -jnp.inf)
    ridx = lax.broadcasted_iota(jnp.int32, (N_EXPERTS, tm), 0)
    taken = jnp.zeros((N_EXPERTS, tm), F32)
    for _ in range(TOP_K):
        taken = _pick_first_max(sel, taken, ridx, N_EXPERTS)
    w = jnp.where(taken > 0, scores, 0.0)
    return w / jnp.sum(w, axis=0, keepdims=True) * ROUTED_SCALE


def _post_mix_kernel(x_ref, ssm_ref, att_ref, ga_ref, wos_ref, woa_ref, gf_ref,
                     wrh_ref, wrl_ref, rb_ref, s1_ref, s3_ref, s2_ref,
                     base_ref, hn_ref, gate_t_ref, cnt_ref):
    att_n = _rms(att_ref[...].astype(F32), ga_ref[...]).astype(BF16)
    h = x_ref[...] + _dot(ssm_ref[...], wos_ref[...]) + _dot(att_n, woa_ref[...])
    hn = _rms(h, gf_ref[...])
    hn_hi = hn.astype(BF16)
    hn_lo = (hn - hn_hi.astype(F32)).astype(BF16)
    hn_ref[...] = hn_hi

    nt = functools.partial(lax.dot_general, dimension_numbers=_NT, preferred_element_type=F32)
    logits = (nt(wrh_ref[...], hn_hi) + nt(wrh_ref[...], hn_lo) + nt(wrl_ref[...], hn_hi))
    scores = jax.nn.sigmoid(logits)
    gate_t = _route(scores, scores + rb_ref[...])
    gate_t_ref[...] = gate_t
    lane = lax.broadcasted_iota(jnp.int32, (1, LANES), 1)
    routed = jnp.where(gate_t > 0.0, 1.0, 0.0)
    counts = jnp.zeros((N_EXPERTS, LANES), F32)
    for blk in range(ROW_TILE // MOE_BLOCK):
        c = jnp.sum(routed[:, blk * MOE_BLOCK:(blk + 1) * MOE_BLOCK], axis=1, keepdims=True)
        counts = jnp.where(lane == blk, c, counts)
    cnt_ref[0] = counts

    a = _dot(hn_hi, s1_ref[...])
    b = _dot(hn_hi, s3_ref[...])
    act = (a * jax.nn.sigmoid(a) * b).astype(BF16)
    base_ref[...] = h + _dot(act, s2_ref[...])


def _post_mix(x2, ssm, att, g_att, wo_s, wo_a, g_ffn, wr_hi, wr_lo, rbias, s1, s3, s2):
    n = x2.shape[0]
    row = lambda i: (i, 0)
    fixed = lambda i: (0, 0)
    full = lambda a: pl.BlockSpec(a.shape, fixed)
    return pl.pallas_call(
        _post_mix_kernel,
        grid=(n // ROW_TILE,),
        in_specs=[
            pl.BlockSpec((ROW_TILE, D_MODEL), row),
            pl.BlockSpec((ROW_TILE, D_SSM), row),
            pl.BlockSpec((ROW_TILE, D_ATT), row),
            full(g_att), full(wo_s), full(wo_a), full(g_ffn),
            full(wr_hi), full(wr_lo), full(rbias), full(s1), full(s3), full(s2),
        ],
        out_specs=[
            pl.BlockSpec((ROW_TILE, D_MODEL), row),
            pl.BlockSpec((ROW_TILE, D_MODEL), row),
            pl.BlockSpec((N_EXPERTS, ROW_TILE), lambda i: (0, i)),
            pl.BlockSpec((1, N_EXPERTS, LANES), lambda i: (i, 0, 0)),
        ],
        out_shape=[
            jax.ShapeDtypeStruct((n, D_MODEL), F32),
            jax.ShapeDtypeStruct((n, D_MODEL), BF16),
            jax.ShapeDtypeStruct((N_EXPERTS, n), F32),
            jax.ShapeDtypeStruct((n // ROW_TILE, N_EXPERTS, LANES), F32),
        ],
        compiler_params=pltpu.CompilerParams(
            dimension_semantics=("parallel",), vmem_limit_bytes=VMEM_LIMIT),
        name="post_mix",
    )(x2, ssm, att, g_att, wo_s, wo_a, g_ffn, wr_hi, wr_lo, rbias, s1, s3, s2)


def _slot_keys(gate_t_ref, earlier_ref, upto_ref, key_ref):
    sel = gate_t_ref[...] > 0.0
    sel_f = jnp.where(sel, 1.0, 0.0)
    rank = _dot(sel_f.astype(BF16), earlier_ref[...])
    count = jnp.sum(sel_f, axis=1, keepdims=True)
    padded = jnp.floor((count + (CHUNK - 1)) * (1.0 / CHUNK)) * CHUNK
    first = _dot(upto_ref[...], jnp.broadcast_to(padded, (N_EXPERTS, LANES)).astype(BF16))
    first = jnp.concatenate([first] * (MOE_BLOCK // LANES), axis=1)
    key_ref[...] = jnp.where(sel, rank + first, -1.0)


def _fill_slot_rows(p_ref, key_ref, chunk_expert_ref, base, n_chunks, value_ref=None):
    within = lax.broadcasted_iota(jnp.int32, (CHUNK, MOE_BLOCK), 0).astype(F32)

    def fill(c, carry):
        e = chunk_expert_ref[base + c]
        hit = (key_ref[pl.ds(e, 1), :] - (c * CHUNK).astype(F32)) == within
        vals = 1.0 if value_ref is None else value_ref[pl.ds(e, 1), :]
        rows = pl.ds(pl.multiple_of(c * CHUNK, CHUNK), CHUNK)
        p_ref[rows, :] = jnp.where(hit, vals, 0.0).astype(p_ref.dtype)
        return carry

    lax.fori_loop(0, n_chunks, fill, 0)


def _chunk_copy(block_rows_ref, global_ref, c, dst, sem, to_global):
    local = block_rows_ref.at[pl.ds(pl.multiple_of(c * CHUNK, CHUNK), CHUNK), :]
    remote = global_ref.at[pl.ds(pl.multiple_of(dst * CHUNK, CHUNK), CHUNK), :]
    return (pltpu.make_async_copy(local, remote, sem) if to_global
            else pltpu.make_async_copy(remote, local, sem))


def _dispatch_kernel(chunk_expert_ref, chunk_dst_ref, n_chunks_ref, pad_dst_ref, n_pad_ref,
                     hn_ref, gate_t_ref, earlier_ref, upto_ref, xs_ref,
                     key_ref, p_ref, rows_ref, zero_ref, sems, zero_sem):
    b = pl.program_id(0)
    last = pl.num_programs(0) - 1
    cur = b % 2
    base = b * BLOCK_CHUNKS
    n_chunks = n_chunks_ref[b]

    @pl.when(b == 0)
    def _():
        p_ref[...] = jnp.zeros_like(p_ref)
        zero_ref[...] = jnp.zeros_like(zero_ref)

        def pad_copy(j):
            dst = pl.ds(pl.multiple_of(pad_dst_ref[j] * CHUNK, CHUNK), CHUNK)
            return pltpu.make_async_copy(zero_ref, xs_ref.at[dst, :], zero_sem)

        def start(j, carry):
            pad_copy(j).start()
            return carry

        def wait(j, carry):
            pad_copy(j).wait()
            return carry

        lax.fori_loop(0, n_pad_ref[0], start, 0)
        lax.fori_loop(0, n_pad_ref[0], wait, 0)

    _slot_keys(gate_t_ref, earlier_ref, upto_ref, key_ref)
    _fill_slot_rows(p_ref, key_ref, chunk_expert_ref, base, n_chunks)

    group = MOE_BLOCK

    def permute(g, carry):
        rows = pl.ds(pl.multiple_of(g * group, group), group)
        rows_ref[cur, rows, :] = _dot(p_ref[rows, :], hn_ref[...]).astype(BF16)
        return carry

    lax.fori_loop(0, (n_chunks * CHUNK + group - 1) // group, permute, 0)

    def send(c, carry):
        _chunk_copy(rows_ref.at[cur], xs_ref, c, chunk_dst_ref[base + c],
                    sems.at[cur], True).start()
        return carry

    lax.fori_loop(0, n_chunks, send, 0)

    def drain(slot, count):
        def wait(c, carry):
            _chunk_copy(rows_ref.at[slot], xs_ref, 0, 0, sems.at[slot], True).wait()
            return carry
        lax.fori_loop(0, count, wait, 0)

    @pl.when(b > 0)
    def _():
        drain(1 - cur, n_chunks_ref[jnp.maximum(b - 1, 0)])

    @pl.when(b == last)
    def _():
        drain(cur, n_chunks)


def _dispatch(chunk_expert, chunk_dst, n_chunks, pad_dst, n_pad, hn, gate_t, earlier_t, upto,
              total_rows):
    nb = hn.shape[0] // MOE_BLOCK
    fixed = lambda i, *_: (0, 0)
    return pl.pallas_call(
        _dispatch_kernel,
        grid_spec=pltpu.PrefetchScalarGridSpec(
            num_scalar_prefetch=5,
            grid=(nb,),
            in_specs=[
                pl.BlockSpec((MOE_BLOCK, D_MODEL), lambda i, *_: (i, 0)),
                pl.BlockSpec((N_EXPERTS, MOE_BLOCK), lambda i, *_: (0, i)),
                pl.BlockSpec(earlier_t.shape, fixed),
                pl.BlockSpec(upto.shape, fixed),
            ],
            out_specs=pl.BlockSpec(memory_space=pl.ANY),
            scratch_shapes=[
                pltpu.VMEM((N_EXPERTS, MOE_BLOCK), F32),
                pltpu.VMEM((BLOCK_SLOTS, MOE_BLOCK), BF16),
                pltpu.VMEM((2, BLOCK_SLOTS, D_MODEL), BF16),
                pltpu.VMEM((CHUNK, D_MODEL), BF16),
                pltpu.SemaphoreType.DMA((2,)),
                pltpu.SemaphoreType.DMA(()),
            ],
        ),
        out_shape=jax.ShapeDtypeStruct((total_rows, D_MODEL), BF16),
        compiler_params=pltpu.CompilerParams(
            dimension_semantics=("arbitrary",), vmem_limit_bytes=VMEM_LIMIT),
        name="moe_dispatch",
    )(chunk_expert, chunk_dst, n_chunks, pad_dst, n_pad, hn, gate_t, earlier_t, upto)


def _expert_kernel(tile_expert_ref, n_tiles_ref, x_ref, w1_ref, w3_ref, w2_ref, y_ref):
    @pl.when(pl.program_id(0) < n_tiles_ref[0])
    def _():
        x = x_ref[...]
        a = _dot(x, w1_ref[0])
        b = _dot(x, w3_ref[0])
        act = (a * jax.nn.sigmoid(a) * b).astype(BF16)
        y_ref[...] = _dot(act, w2_ref[0]).astype(BF16)


def _experts(tile_expert, n_tiles, xs, w1, w3, w2):
    rows = lambda m, te, nt: (jnp.minimum(m, nt[0] - 1), 0)
    weight = lambda m, te, nt: (te[m], 0, 0)
    return pl.pallas_call(
        _expert_kernel,
        grid_spec=pltpu.PrefetchScalarGridSpec(
            num_scalar_prefetch=2,
            grid=(xs.shape[0] // EXPERT_ROWS,),
            in_specs=[
                pl.BlockSpec((EXPERT_ROWS, D_MODEL), rows),
                pl.BlockSpec((1, D_MODEL, D_EXPERT), weight),
                pl.BlockSpec((1, D_MODEL, D_EXPERT), weight),
                pl.BlockSpec((1, D_EXPERT, D_MODEL), weight),
            ],
            out_specs=pl.BlockSpec((EXPERT_ROWS, D_MODEL), rows),
        ),
        out_shape=jax.ShapeDtypeStruct(xs.shape, BF16),
        compiler_params=pltpu.CompilerParams(
            dimension_semantics=("arbitrary",), vmem_limit_bytes=VMEM_LIMIT),
        name="moe_experts",
    )(tile_expert, n_tiles, xs, w1, w3, w2)


def _combine_kernel(chunk_expert_ref, chunk_dst_ref, n_chunks_ref,
                    gate_t_ref, earlier_ref, upto_ref, ys_ref, base_ref, o_ref,
                    key_ref, p_ref, rows_ref, sems):
    b = pl.program_id(0)
    last = pl.num_programs(0) - 1
    cur = b % 2
    n_chunks = n_chunks_ref[b]

    def fetch(blk, slot):
        def start(c, carry):
            _chunk_copy(rows_ref.at[slot], ys_ref, c, chunk_dst_ref[blk * BLOCK_CHUNKS + c],
                        sems.at[slot], False).start()
            return carry
        lax.fori_loop(0, n_chunks_ref[blk], start, 0)

    @pl.when(b == 0)
    def _():
        rows_ref[...] = jnp.zeros_like(rows_ref)
        fetch(0, 0)

    @pl.when(b < last)
    def _():
        fetch(jnp.minimum(b + 1, last), 1 - cur)

    _slot_keys(gate_t_ref, earlier_ref, upto_ref, key_ref)
    p_ref[...] = jnp.zeros_like(p_ref)
    _fill_slot_rows(p_ref, key_ref, chunk_expert_ref, b * BLOCK_CHUNKS, n_chunks, gate_t_ref)

    def wait(c, carry):
        _chunk_copy(rows_ref.at[cur], ys_ref, 0, 0, sems.at[cur], False).wait()
        return carry

    lax.fori_loop(0, n_chunks, wait, 0)

    o_ref[...] = base_ref[...]
    group = MOE_BLOCK

    def gather(g, carry):
        rows = pl.ds(pl.multiple_of(g * group, group), group)
        weights = p_ref[rows, :].T.astype(BF16)
        o_ref[...] += _dot(weights, rows_ref[cur, rows, :])
        return carry

    lax.fori_loop(0, (n_chunks * CHUNK + group - 1) // group, gather, 0)


def _combine(chunk_expert, chunk_dst, n_chunks, gate_t, earlier_t, upto, ys, base):
    n = base.shape[0]
    fixed = lambda i, *_: (0, 0)
    row = lambda i, *_: (i, 0)
    return pl.pallas_call(
        _combine_kernel,
        grid_spec=pltpu.PrefetchScalarGridSpec(
            num_scalar_prefetch=3,
            grid=(n // MOE_BLOCK,),
            in_specs=[
                pl.BlockSpec((N_EXPERTS, MOE_BLOCK), lambda i, *_: (0, i)),
                pl.BlockSpec(earlier_t.shape, fixed),
                pl.BlockSpec(upto.shape, fixed),
                pl.BlockSpec(memory_space=pl.ANY),
                pl.BlockSpec((MOE_BLOCK, D_MODEL), row),
            ],
            out_specs=pl.BlockSpec((MOE_BLOCK, D_MODEL), row),
            scratch_shapes=[
                pltpu.VMEM((N_EXPERTS, MOE_BLOCK), F32),
                pltpu.VMEM((BLOCK_SLOTS, MOE_BLOCK), F32),
                pltpu.VMEM((2, BLOCK_SLOTS, D_MODEL), BF16),
                pltpu.SemaphoreType.DMA((2,)),
            ],
        ),
        out_shape=jax.ShapeDtypeStruct((n, D_MODEL), F32),
        compiler_params=pltpu.CompilerParams(
            dimension_semantics=("arbitrary",), vmem_limit_bytes=VMEM_LIMIT),
        name="moe_combine",
    )(chunk_expert, chunk_dst, n_chunks, gate_t, earlier_t, upto, ys, base)


def _moe_layout(counts):
    nb = counts.shape[0]
    chunks = (counts + (CHUNK - 1)) // CHUNK
    first = jnp.cumsum(chunks, axis=1) - chunks
    n_chunks = jnp.sum(chunks, axis=1)
    tile_chunks = EXPERT_ROWS // CHUNK
    expert_tiles = (jnp.sum(chunks, axis=0) + tile_chunks - 1) // tile_chunks
    expert_first = (jnp.cumsum(expert_tiles) - expert_tiles) * tile_chunks
    dst = expert_first[None, :] + jnp.cumsum(chunks, axis=0) - chunks
    c = jnp.arange(BLOCK_CHUNKS)
    last = first + chunks
    chunk_expert = jnp.minimum(jnp.sum(c[None, :, None] >= last[:, None, :], axis=2),
                               N_EXPERTS - 1)
    pick = lambda a: jnp.take_along_axis(a, chunk_expert, axis=1)
    chunk_dst = pick(dst) + c[None, :] - pick(first)
    max_tiles = nb * BLOCK_CHUNKS // tile_chunks + N_EXPERTS
    tile_expert = jnp.minimum(
        jnp.sum(jnp.arange(max_tiles)[:, None] >= jnp.cumsum(expert_tiles)[None, :], axis=1),
        N_EXPERTS - 1)
    expert_chunks = jnp.sum(chunks, axis=0)
    pad = expert_tiles * tile_chunks - expert_chunks
    j = jnp.arange(N_EXPERTS * tile_chunks)
    pad_expert = jnp.minimum(jnp.sum(j[:, None] >= jnp.cumsum(pad)[None, :], axis=1), N_EXPERTS - 1)
    pad_dst = (expert_first + expert_chunks)[pad_expert] + j - (jnp.cumsum(pad) - pad)[pad_expert]
    i32 = lambda a: a.astype(jnp.int32)
    return (i32(chunk_expert).reshape(-1), i32(chunk_dst).reshape(-1), i32(n_chunks),
            i32(pad_dst), i32(jnp.sum(pad)).reshape(1),
            i32(tile_expert), i32(jnp.sum(expert_tiles)).reshape(1), max_tiles * EXPERT_ROWS)


def _layer(h, g_mix, w_in, lam_re, lam_im, b_re, b_im, c_re, c_im, d_skip, log_step,
           w_glu, b_glu, g_q, g_k, g_out_ssm, g_out_att, w_out, g_ffn, w_router,
           router_bias, e_w1, e_w3, e_w2, s_w1, s_w3, s_w2):
    bsz, seq, d = h.shape
    n = bsz * seq
    x2 = h.reshape(n, d)
    n_heads = D_ATT // HEAD_DIM

    head_id = jnp.arange(D_ATT) // HEAD_DIM
    hsum = jnp.where(head_id[:, None] == head_id[None, :], 1.0 / HEAD_DIM, 0.0).astype(BF16)
    ids = jnp.arange(ATT_TILE)
    later = (ids[:, None] > ids[None, :]).astype(BF16)
    zero, one = jnp.zeros_like(later), jnp.ones_like(later)
    tri = jnp.block([[later, zero, one, zero], [zero, later, zero, one]])

    gq = (jnp.tile(g_q, n_heads) * (math.log2(math.e) / math.sqrt(HEAD_DIM))).reshape(1, D_ATT)
    gk = jnp.tile(g_k, n_heads).reshape(1, D_ATT)
    u, q, k, v = _in_proj(x2, g_mix.reshape(1, d), w_in.astype(BF16), gq, gk, hsum)

    u_tb = u.reshape(bsz, seq, D_SSM).transpose(1, 0, 2).reshape(n, D_SSM)
    s5w = _s5_weights(lam_re, lam_im, b_re, b_im, c_re, c_im, log_step)
    ssm_tb = _s5(u_tb, *s5w, d_skip.reshape(1, D_SSM), w_glu.astype(BF16),
                 b_glu.reshape(1, D_SSM), g_out_ssm.reshape(1, D_SSM))
    ssm = ssm_tb.reshape(seq, bsz, D_SSM).transpose(1, 0, 2).reshape(n, D_SSM)

    att = _attention(q.reshape(bsz, seq, D_ATT), k.reshape(bsz, seq, D_ATT),
                     v.reshape(bsz, seq, D_ATT), tri).reshape(n, D_ATT)

    wr_t = w_router.T
    wr_hi = wr_t.astype(BF16)
    wr_lo = (wr_t - wr_hi.astype(F32)).astype(BF16)
    w_out_b = w_out.astype(BF16)
    base, hn, gate_t, counts = _post_mix(
        x2, ssm, att, g_out_att.reshape(1, D_ATT), w_out_b[:D_SSM], w_out_b[D_SSM:],
        g_ffn.reshape(1, d), wr_hi, wr_lo, router_bias.reshape(N_EXPERTS, 1),
        s_w1.astype(BF16), s_w3.astype(BF16), s_w2.astype(BF16))

    w1, w3, w2 = e_w1.astype(BF16), e_w3.astype(BF16), e_w2.astype(BF16)
    tok = jnp.arange(MOE_BLOCK)
    earlier_t = (tok[:, None] < tok[None, :]).astype(BF16)
    exp = jnp.arange(N_EXPERTS)
    upto = (exp[None, :] < exp[:, None]).astype(BF16)
    blocks_per_tile = ROW_TILE // MOE_BLOCK
    counts = jnp.round(counts[:, :, :blocks_per_tile]).astype(jnp.int32)
    counts = counts.transpose(0, 2, 1).reshape(n // MOE_BLOCK, N_EXPERTS)
    (chunk_expert, chunk_dst, n_chunks, pad_dst, n_pad, tile_expert, n_tiles,
     total_rows) = _moe_layout(counts)
    xs = _dispatch(chunk_expert, chunk_dst, n_chunks, pad_dst, n_pad, hn, gate_t, earlier_t, upto,
                   total_rows)
    ys = _experts(tile_expert, n_tiles, xs, w1, w3, w2)
    out = _combine(chunk_expert, chunk_dst, n_chunks, gate_t, earlier_t, upto, ys, base)
    return out.reshape(bsz, seq, d)


def kernel(x, g_mix, w_in, lam_re, lam_im, b_re, b_im, c_re, c_im, d_skip, log_step,
           w_glu, b_glu, g_q, g_k, g_out_ssm, g_out_att, w_out, g_ffn, w_router,
           router_bias, e_w1, e_w3, e_w2, s_w1, s_w3, s_w2):
    h = x
    for l in range(g_mix.shape[0]):
        h = _layer(h, g_mix[l], w_in[l], lam_re[l], lam_im[l], b_re[l], b_im[l], c_re[l],
                   c_im[l], d_skip[l], log_step[l], w_glu[l], b_glu[l], g_q[l], g_k[l],
                   g_out_ssm[l], g_out_att[l], w_out[l], g_ffn[l], w_router[l],
                   router_bias[l], e_w1[l], e_w3[l], e_w2[l], s_w1[l], s_w3[l], s_w2[l])
    return h
```

```python
import functools
import math

import jax
import jax.numpy as jnp
from jax import lax
from jax.experimental import pallas as pl
from jax.experimental.pallas import tpu as pltpu

F32 = jnp.float32
BF16 = jnp.bfloat16

D_MODEL = 1024
D_SSM = 512
SSM_GROUP = 16
N_SSM_GROUPS = 32
SSM_STATE = 64
D_ATT = 512
HEAD_DIM = 64
N_EXPERTS = 64
TOP_K = 8
N_EXPERT_GROUPS = 8
TOPK_GROUPS = 4
D_EXPERT = 256
ROUTED_SCALE = 2.5
EPS = 1e-6
LAMBDA_RE_MAX = -1e-4

LANES = 128
SUBLANES = 8
VMEM_LIMIT = 56 * 1024 * 1024

N_STATE = N_SSM_GROUPS * SSM_STATE
GROUPS_PER_TILE = LANES // SSM_GROUP
N_LANE_TILES = D_SSM // LANES
STATE_PER_TILE = GROUPS_PER_TILE * SSM_STATE

ROW_TILE = 512
S5_CHUNK = 64
S5_COLS = 1024
ATT_TILE = LANES
EXP2_UNDERFLOW = 174.0
MOE_BLOCK = 256
CHUNK = 16
BLOCK_CHUNKS = MOE_BLOCK * TOP_K // CHUNK + N_EXPERTS
BLOCK_SLOTS = BLOCK_CHUNKS * CHUNK
DISPATCH_ROWS = 1024
EXPERT_ROWS = 1024
LOOP_UNROLL = 4

_NT = (((1,), (1,)), ((), ()))


def _rms(x, g):
    ms = jnp.mean(x * x, axis=-1, keepdims=True)
    return x * lax.rsqrt(ms + EPS) * g


def _dot(a, b):
    return jnp.dot(a, b, preferred_element_type=F32)


def _in_proj_kernel(x_ref, g_ref, w_ref, gq_ref, gk_ref, hsum_ref,
                    u_ref, q_ref, k_ref, v_ref):
    hn = _rms(x_ref[...], g_ref[...]).astype(BF16)
    u_ref[...] = _dot(hn, w_ref[:, 0:D_SSM]).astype(BF16)
    v_ref[...] = _dot(hn, w_ref[:, D_SSM + 2 * D_ATT:]).astype(BF16)

    def head_norm(lo, g):
        y = _dot(hn, w_ref[:, lo:lo + D_ATT])
        ms = _dot((y * y).astype(BF16), hsum_ref[...])
        return (y * lax.rsqrt(ms + EPS) * g).astype(BF16)

    q_ref[...] = head_norm(D_SSM, gq_ref[...])
    k_ref[...] = head_norm(D_SSM + D_ATT, gk_ref[...])


def _in_proj(x2, g_mix, w_in, gq, gk, hsum):
    n = x2.shape[0]
    row = lambda i: (i, 0)
    fixed = lambda i: (0, 0)
    out = jax.ShapeDtypeStruct((n, D_SSM), BF16)
    return pl.pallas_call(
        _in_proj_kernel,
        grid=(n // ROW_TILE,),
        in_specs=[
            pl.BlockSpec((ROW_TILE, D_MODEL), row),
            pl.BlockSpec((1, D_MODEL), fixed),
            pl.BlockSpec(w_in.shape, fixed),
            pl.BlockSpec((1, D_ATT), fixed),
            pl.BlockSpec((1, D_ATT), fixed),
            pl.BlockSpec((D_ATT, D_ATT), fixed),
        ],
        out_specs=[pl.BlockSpec((ROW_TILE, D_SSM), row)] * 4,
        out_shape=[out] * 4,
        compiler_params=pltpu.CompilerParams(
            dimension_semantics=("parallel",), vmem_limit_bytes=VMEM_LIMIT),
        name="in_proj",
    )(x2, g_mix, w_in, gq, gk, hsum)


def _gelu_tanh(x):
    c = math.sqrt(2.0 / math.pi)
    return 0.5 * x * (1.0 + jnp.tanh(c * (x + 0.044715 * (x * x * x))))


def _s5_kernel(u_ref, bre_ref, bim_ref, lr_ref, li_ref, cre_ref, cimn_ref,
               d_ref, wg_ref, bg_ref, go_ref, o_ref, sre, sim, hre, him, tb):
    @pl.when(pl.program_id(0) == 0)
    def _():
        hre[...] = jnp.zeros_like(hre)
        him[...] = jnp.zeros_like(him)

    n_batch = u_ref.shape[0]
    tiles = [slice(a * LANES, (a + 1) * LANES) for a in range(N_LANE_TILES)]
    for b in range(n_batch):
        ub = u_ref[b].astype(F32)
        for a, lanes in enumerate(tiles):
            tb[a, pl.ds(b, S5_CHUNK, stride=n_batch), :] = ub[:, lanes]
    u_f32 = jnp.concatenate([tb[a] for a in range(N_LANE_TILES)], axis=1)
    u = u_f32.astype(BF16)
    for a in range(N_LANE_TILES):
        ua = u[:, a * LANES:(a + 1) * LANES]
        cols = slice(a * STATE_PER_TILE, (a + 1) * STATE_PER_TILE)
        sre[:, cols] = _dot(ua, bre_ref[a])
        sim[:, cols] = _dot(ua, bim_ref[a])

    for cb in range(N_STATE // S5_COLS):
        cols = slice(cb * S5_COLS, (cb + 1) * S5_COLS)
        lr = jnp.broadcast_to(lr_ref[:, cols], (SUBLANES, S5_COLS))
        li = jnp.broadcast_to(li_ref[:, cols], (SUBLANES, S5_COLS))

        def step(t, h, cols=cols, lr=lr, li=li):
            hr, hi = h
            rows = pl.ds(pl.multiple_of(t * SUBLANES, SUBLANES), SUBLANES)
            nr = lr * hr - li * hi + sre[rows, cols]
            ni = lr * hi + li * hr + sim[rows, cols]
            sre[rows, cols] = nr
            sim[rows, cols] = ni
            return nr, ni

        hr, hi = lax.fori_loop(0, S5_CHUNK, step, (hre[:, cols], him[:, cols]),
                               unroll=2)
        hre[:, cols] = hr
        him[:, cols] = hi

    ys = []
    for a in range(N_LANE_TILES):
        cols = slice(a * STATE_PER_TILE, (a + 1) * STATE_PER_TILE)
        ys.append(_dot(sre[:, cols].astype(BF16), cre_ref[a])
                  + _dot(sim[:, cols].astype(BF16), cimn_ref[a]))
    y = jnp.concatenate(ys, axis=1)
    y = _gelu_tanh(y + d_ref[...] * u_f32)
    y = y * jax.nn.sigmoid(_dot(y.astype(BF16), wg_ref[...]) + bg_ref[...])
    y = _rms(y, go_ref[...])
    for a, lanes in enumerate(tiles):
        tb[a] = y[:, lanes]
    for b in range(n_batch):
        o_ref[b] = jnp.concatenate(
            [tb[a, pl.ds(b, S5_CHUNK, stride=n_batch), :] for a in range(N_LANE_TILES)],
            axis=1).astype(BF16)


def _s5(u, bre, bim, lr, li, cre, cimn, d_skip, w_glu, b_glu, g_out):
    bsz, seq, _ = u.shape
    rows = S5_CHUNK * bsz
    fixed2 = lambda i: (0, 0)
    fixed3 = lambda i: (0, 0, 0)
    return pl.pallas_call(
        _s5_kernel,
        grid=(seq // S5_CHUNK,),
        in_specs=[
            pl.BlockSpec((bsz, S5_CHUNK, D_SSM), lambda i: (0, i, 0)),
            pl.BlockSpec(bre.shape, fixed3),
            pl.BlockSpec(bim.shape, fixed3),
            pl.BlockSpec((1, N_STATE), fixed2),
            pl.BlockSpec((1, N_STATE), fixed2),
            pl.BlockSpec(cre.shape, fixed3),
            pl.BlockSpec(cimn.shape, fixed3),
            pl.BlockSpec((1, D_SSM), fixed2),
            pl.BlockSpec((D_SSM, D_SSM), fixed2),
            pl.BlockSpec((1, D_SSM), fixed2),
            pl.BlockSpec((1, D_SSM), fixed2),
        ],
        out_specs=pl.BlockSpec((bsz, S5_CHUNK, D_SSM), lambda i: (0, i, 0)),
        out_shape=jax.ShapeDtypeStruct(u.shape, BF16),
        scratch_shapes=[
            pltpu.VMEM((rows, N_STATE), F32),
            pltpu.VMEM((rows, N_STATE), F32),
            pltpu.VMEM((SUBLANES, N_STATE), F32),
            pltpu.VMEM((SUBLANES, N_STATE), F32),
            pltpu.VMEM((N_LANE_TILES, rows, LANES), F32),
        ],
        compiler_params=pltpu.CompilerParams(
            dimension_semantics=("arbitrary",), vmem_limit_bytes=VMEM_LIMIT),
        name="s5_mixer",
    )(u, bre, bim, lr, li, cre, cimn, d_skip, w_glu, b_glu, g_out)


def _s5_weights(lam_re, lam_im, b_re, b_im, c_re, c_im, log_step):
    lre = jnp.minimum(lam_re, LAMBDA_RE_MAX)
    step = jnp.exp(log_step)[:, None]
    mag = jnp.exp(lre * step)
    bar_re = mag * jnp.cos(lam_im * step)
    bar_im = mag * jnp.sin(lam_im * step)
    den = lre * lre + lam_im * lam_im
    coef_re = ((bar_re - 1.0) * lre + bar_im * lam_im) / den
    coef_im = (bar_im * lre - (bar_re - 1.0) * lam_im) / den
    bbar_re = coef_re[..., None] * b_re - coef_im[..., None] * b_im
    bbar_im = coef_re[..., None] * b_im + coef_im[..., None] * b_re
    eye = jnp.eye(GROUPS_PER_TILE, dtype=F32)

    def pack_b(b):
        b = b.reshape(N_LANE_TILES, GROUPS_PER_TILE, SSM_STATE, SSM_GROUP)
        m = jnp.einsum('agph,gk->aghkp', b, eye)
        return m.reshape(N_LANE_TILES, LANES, STATE_PER_TILE).astype(BF16)

    def pack_c(c):
        c = c.reshape(N_LANE_TILES, GROUPS_PER_TILE, SSM_GROUP, SSM_STATE)
        m = jnp.einsum('aghp,gk->agpkh', c, eye)
        return m.reshape(N_LANE_TILES, STATE_PER_TILE, LANES).astype(BF16)

    return (pack_b(bbar_re), pack_b(bbar_im),
            bar_re.reshape(1, N_STATE), bar_im.reshape(1, N_STATE),
            pack_c(c_re), pack_c(-c_im))


def _attn_kernel(q_ref, k_ref, v_ref, tri_ref, o_ref, *scratch):
    acc_refs, car_refs = scratch[:D_ATT // LANES], scratch[D_ATT // LANES:]
    t = ATT_TILE
    i = pl.program_id(1)
    first_head = lax.broadcasted_iota(jnp.int32, (1, LANES), 1) < HEAD_DIM

    def split_heads(x):
        zero = jnp.zeros_like(x)
        return jnp.concatenate([jnp.where(first_head, x, zero),
                                jnp.where(first_head, zero, x)], axis=0)

    def block(j, on_diagonal):
        start = pl.multiple_of(j * t, t)
        if on_diagonal:
            col = lax.broadcasted_iota(jnp.int32, (t, 2 * t), 1)
            row = lax.broadcasted_iota(jnp.int32, (t, 2 * t), 0)
            strictly_before = jnp.where(col >= t, col - t, col) < row
        tiles = [slice(p * LANES, (p + 1) * LANES) for p in range(D_ATT // LANES)]
        zs = [lax.dot_general(q_ref[0, :, lanes], split_heads(k_ref[0, pl.ds(start, t), lanes]),
                              _NT, preferred_element_type=F32) for lanes in tiles]
        sps = [jnp.maximum(z, 0.0) + jnp.log2(1.0 + jnp.exp2(-jnp.abs(z))) for z in zs]
        if on_diagonal:
            sps = [jnp.where(strictly_before, sp, 0.0) for sp in sps]
        sums = [_dot(sp.astype(BF16), tri_ref[...]) for sp in sps]
        smallest = None
        for z, sp, sm, lanes, acc_ref, car_ref in zip(zs, sps, sums, tiles, acc_refs, car_refs):
            car = car_ref[...]
            w = jnp.exp2(z - sp - (sm[:, :2 * t] + car))
            if on_diagonal:
                w = jnp.where(strictly_before, w, 0.0)
            acc_ref[...] += _dot(w.astype(BF16), split_heads(v_ref[0, pl.ds(start, t), lanes]))
            car = car + sm[:, 2 * t:]
            car_ref[...] = car
            smallest = car if smallest is None else jnp.minimum(smallest, car)
        return jnp.min(smallest)

    for ref in scratch:
        ref[...] = jnp.zeros_like(ref)
    carried = block(i, True)

    def more(state):
        j, carried = state
        return (j >= 0) & (carried < EXP2_UNDERFLOW)

    def earlier(state):
        j, _ = state
        return j - 1, block(j, False)

    lax.while_loop(more, earlier, (i - 1, carried))
    o_ref[0] = jnp.concatenate(
        [ref[...] for ref in acc_refs], axis=1).astype(o_ref.dtype)


def _attention(q, k, v, tri):
    bsz, seq, _ = q.shape
    t = ATT_TILE
    n_tiles = D_ATT // LANES
    return pl.pallas_call(
        _attn_kernel,
        grid=(bsz, seq // t),
        in_specs=[
            pl.BlockSpec((1, t, D_ATT), lambda b, i: (b, i, 0)),
            pl.BlockSpec((1, seq, D_ATT), lambda b, i: (b, 0, 0)),
            pl.BlockSpec((1, seq, D_ATT), lambda b, i: (b, 0, 0)),
            pl.BlockSpec(tri.shape, lambda b, i: (0, 0)),
        ],
        out_specs=pl.BlockSpec((1, t, D_ATT), lambda b, i: (b, i, 0)),
        out_shape=jax.ShapeDtypeStruct((bsz, seq, D_ATT), BF16),
        scratch_shapes=([pltpu.VMEM((t, LANES), F32)] * n_tiles
                        + [pltpu.VMEM((t, 2 * t), F32)] * n_tiles),
        compiler_params=pltpu.CompilerParams(
            dimension_semantics=("parallel", "parallel"),
            vmem_limit_bytes=VMEM_LIMIT),
        name="stick_breaking_attention",
    )(q, k, v, tri)


def _pick_first_max(vals, taken, ridx, n):
    cand = jnp.where(taken > 0, -jnp.inf, vals)
    m = jnp.max(cand, axis=0, keepdims=True)
    first = jnp.min(jnp.where((cand == m) & (taken == 0), ridx, n), axis=0, keepdims=True)
    return jnp.where(ridx == first, 1.0, taken)


def _route(scores, biased):
    tm = scores.shape[1]
    per_group = N_EXPERTS // N_EXPERT_GROUPS
    ridx8 = lax.broadcasted_iota(jnp.int32, (per_group, tm), 0)
    zeros8 = jnp.zeros((per_group, tm), F32)
    group_scores = []
    for g in range(N_EXPERT_GROUPS):
        blk = biased[g * per_group:(g + 1) * per_group]
        top1 = _pick_first_max(blk, zeros8, ridx8, per_group)
        top2 = _pick_first_max(blk, top1, ridx8, per_group)
        group_scores.append(jnp.sum(jnp.where(top2 > 0, blk, 0.0), axis=0, keepdims=True))
    grp = jnp.concatenate(group_scores, axis=0)
    ridx_g = lax.broadcasted_iota(jnp.int32, (N_EXPERT_GROUPS, tm), 0)
    grp_taken = jnp.zeros((N_EXPERT_GROUPS, tm), F32)
    for _ in range(TOPK_GROUPS):
        grp_taken = _pick_first_max(grp, grp_taken, ridx_g, N_EXPERT_GROUPS)
    expert_mask = jnp.concatenate(
        [jnp.broadcast_to(grp_taken[g:g + 1], (per_group, tm)) for g in range(N_EXPERT_GROUPS)],
        axis=0)
    sel = jnp.where(expert_mask > 0, biased, -jnp.inf)
    ridx = lax.broadcasted_iota(jnp.int32, (N_EXPERTS, tm), 0)
    taken = jnp.zeros((N_EXPERTS, tm), F32)
    for _ in range(TOP_K):
        taken = _pick_first_max(sel, taken, ridx, N_EXPERTS)
    w = jnp.where(taken > 0, scores, 0.0)
    return w / jnp.sum(w, axis=0, keepdims=True) * ROUTED_SCALE


def _post_mix_kernel(x_ref, ssm_ref, att_ref, ga_ref, wos_ref, woa_ref, gf_ref,
                     wrh_ref, wrl_ref, rb_ref, s1_ref, s3_ref, s2_ref,
                     base_ref, hn_ref, gate_t_ref, cnt_ref):
    att_n = _rms(att_ref[...].astype(F32), ga_ref[...]).astype(BF16)
    h = x_ref[...] + _dot(ssm_ref[...], wos_ref[...]) + _dot(att_n, woa_ref[...])
    hn = _rms(h, gf_ref[...])
    hn_hi = hn.astype(BF16)
    hn_lo = (hn - hn_hi.astype(F32)).astype(BF16)
    hn_ref[...] = hn_hi

    nt = functools.partial(lax.dot_general, dimension_numbers=_NT, preferred_element_type=F32)
    logits = (nt(wrh_ref[...], hn_hi) + nt(wrh_ref[...], hn_lo) + nt(wrl_ref[...], hn_hi))
    scores = jax.nn.sigmoid(logits)
    gate_t = _route(scores, scores + rb_ref[...])
    gate_t_ref[...] = gate_t
    lane = lax.broadcasted_iota(jnp.int32, (1, LANES), 1)
    routed = jnp.where(gate_t > 0.0, 1.0, 0.0)
    counts = jnp.zeros((N_EXPERTS, LANES), F32)
    for blk in range(ROW_TILE // MOE_BLOCK):
        c = jnp.sum(routed[:, blk * MOE_BLOCK:(blk + 1) * MOE_BLOCK], axis=1, keepdims=True)
        counts = jnp.where(lane == blk, c, counts)
    cnt_ref[0] = counts

    a = _dot(hn_hi, s1_ref[...])
    b = _dot(hn_hi, s3_ref[...])
    act = (a * jax.nn.sigmoid(a) * b).astype(BF16)
    base_ref[...] = h + _dot(act, s2_ref[...])


def _post_mix(x2, ssm, att, g_att, wo_s, wo_a, g_ffn, wr_hi, wr_lo, rbias, s1, s3, s2):
    n = x2.shape[0]
    row = lambda i: (i, 0)
    fixed = lambda i: (0, 0)
    full = lambda a: pl.BlockSpec(a.shape, fixed)
    return pl.pallas_call(
        _post_mix_kernel,
        grid=(n // ROW_TILE,),
        in_specs=[
            pl.BlockSpec((ROW_TILE, D_MODEL), row),
            pl.BlockSpec((ROW_TILE, D_SSM), row),
            pl.BlockSpec((ROW_TILE, D_ATT), row),
            full(g_att), full(wo_s), full(wo_a), full(g_ffn),
            full(wr_hi), full(wr_lo), full(rbias), full(s1), full(s3), full(s2),
        ],
        out_specs=[
            pl.BlockSpec((ROW_TILE, D_MODEL), row),
            pl.BlockSpec((ROW_TILE, D_MODEL), row),
            pl.BlockSpec((N_EXPERTS, ROW_TILE), lambda i: (0, i)),
            pl.BlockSpec((1, N_EXPERTS, LANES), lambda i: (i, 0, 0)),
        ],
        out_shape=[
            jax.ShapeDtypeStruct((n, D_MODEL), F32),
            jax.ShapeDtypeStruct((n, D_MODEL), BF16),
            jax.ShapeDtypeStruct((N_EXPERTS, n), F32),
            jax.ShapeDtypeStruct((n // ROW_TILE, N_EXPERTS, LANES), F32),
        ],
        compiler_params=pltpu.CompilerParams(
            dimension_semantics=("parallel",), vmem_limit_bytes=VMEM_LIMIT),
        name="post_mix",
    )(x2, ssm, att, g_att, wo_s, wo_a, g_ffn, wr_hi, wr_lo, rbias, s1, s3, s2)


def _slot_keys(gate_t_ref, earlier_ref, upto_ref, key_ref):
    sel = gate_t_ref[...] > 0.0
    sel_f = jnp.where(sel, 1.0, 0.0)
    rank = _dot(sel_f.astype(BF16), earlier_ref[...])
    count = jnp.sum(sel_f, axis=1, keepdims=True)
    padded = jnp.floor((count + (CHUNK - 1)) * (1.0 / CHUNK)) * CHUNK
    first = _dot(upto_ref[...], jnp.broadcast_to(padded, (N_EXPERTS, LANES)).astype(BF16))
    first = jnp.concatenate([first] * (MOE_BLOCK // LANES), axis=1)
    key_ref[...] = jnp.where(sel, rank + first, -1.0)


def _for_each(n, body):
    trips = n // LOOP_UNROLL

    def unrolled(i, carry):
        for u in range(LOOP_UNROLL):
            body(i * LOOP_UNROLL + u)
        return carry

    def single(i, carry):
        body(i)
        return carry

    lax.fori_loop(0, trips, unrolled, 0)
    lax.fori_loop(trips * LOOP_UNROLL, n, single, 0)


def _fill_slot_rows(p_ref, key_ref, chunk_expert_ref, base, n_chunks, value_ref=None):
    within = lax.broadcasted_iota(jnp.int32, (CHUNK, MOE_BLOCK), 0).astype(F32)

    def fill(c):
        e = chunk_expert_ref[base + c]
        hit = (key_ref[pl.ds(e, 1), :] - (c * CHUNK).astype(F32)) == within
        vals = 1.0 if value_ref is None else value_ref[pl.ds(e, 1), :]
        rows = pl.ds(pl.multiple_of(c * CHUNK, CHUNK), CHUNK)
        p_ref[rows, :] = jnp.where(hit, vals, 0.0).astype(p_ref.dtype)

    _for_each(n_chunks, fill)


def _rows_copy(block_rows_ref, global_ref, c, dst, n, sem, to_global):
    local = block_rows_ref.at[pl.ds(pl.multiple_of(c * CHUNK, CHUNK), n * CHUNK), :]
    remote = global_ref.at[pl.ds(pl.multiple_of(dst * CHUNK, CHUNK), n * CHUNK), :]
    return (pltpu.make_async_copy(local, remote, sem) if to_global
            else pltpu.make_async_copy(remote, local, sem))


def _wait_chunks(block_rows_ref, global_ref, n_chunks, sem, to_global):
    bit = 1 << (BLOCK_CHUNKS.bit_length() - 1)
    while bit:
        @pl.when((n_chunks & bit) != 0)
        def _(bit=bit):
            _rows_copy(block_rows_ref, global_ref, 0, 0, bit, sem, to_global).wait()
        bit >>= 1


def _dispatch_kernel(chunk_expert_ref, chunk_dst_ref, n_chunks_ref, pad_dst_ref, n_pad_ref,
                     hn_ref, gate_t_ref, earlier_ref, upto_ref, xs_ref,
                     key_ref, p_ref, rows_ref, zero_ref, sems, zero_sem):
    b = pl.program_id(0)
    last = pl.num_programs(0) - 1
    cur = b % 2
    base = b * BLOCK_CHUNKS
    n_chunks = n_chunks_ref[b]

    @pl.when(b == 0)
    def _():
        p_ref[...] = jnp.zeros_like(p_ref)
        zero_ref[...] = jnp.zeros_like(zero_ref)

        def pad_copy(j):
            dst = pl.ds(pl.multiple_of(pad_dst_ref[j] * CHUNK, CHUNK), CHUNK)
            return pltpu.make_async_copy(zero_ref, xs_ref.at[dst, :], zero_sem)

        _for_each(n_pad_ref[0], lambda j: pad_copy(j).start())
        _for_each(n_pad_ref[0], lambda j: pad_copy(j).wait())

    _slot_keys(gate_t_ref, earlier_ref, upto_ref, key_ref)
    _fill_slot_rows(p_ref, key_ref, chunk_expert_ref, base, n_chunks)

    group = DISPATCH_ROWS

    def permute(g, carry):
        rows = pl.ds(pl.multiple_of(g * group, group), group)
        rows_ref[cur, rows, :] = _dot(p_ref[rows, :], hn_ref[...]).astype(BF16)
        return carry

    lax.fori_loop(0, (n_chunks * CHUNK + group - 1) // group, permute, 0)

    _for_each(n_chunks, lambda c: _rows_copy(
        rows_ref.at[cur], xs_ref, c, chunk_dst_ref[base + c], 1, sems.at[cur], True).start())

    @pl.when(b > 0)
    def _():
        _wait_chunks(rows_ref.at[1 - cur], xs_ref, n_chunks_ref[jnp.maximum(b - 1, 0)],
                     sems.at[1 - cur], True)

    @pl.when(b == last)
    def _():
        _wait_chunks(rows_ref.at[cur], xs_ref, n_chunks, sems.at[cur], True)


def _dispatch(chunk_expert, chunk_dst, n_chunks, pad_dst, n_pad, hn, gate_t, earlier_t, upto,
              total_rows):
    nb = hn.shape[0] // MOE_BLOCK
    fixed = lambda i, *_: (0, 0)
    return pl.pallas_call(
        _dispatch_kernel,
        grid_spec=pltpu.PrefetchScalarGridSpec(
            num_scalar_prefetch=5,
            grid=(nb,),
            in_specs=[
                pl.BlockSpec((MOE_BLOCK, D_MODEL), lambda i, *_: (i, 0)),
                pl.BlockSpec((N_EXPERTS, MOE_BLOCK), lambda i, *_: (0, i)),
                pl.BlockSpec(earlier_t.shape, fixed),
                pl.BlockSpec(upto.shape, fixed),
            ],
            out_specs=pl.BlockSpec(memory_space=pl.ANY),
            scratch_shapes=[
                pltpu.VMEM((N_EXPERTS, MOE_BLOCK), F32),
                pltpu.VMEM((BLOCK_SLOTS, MOE_BLOCK), BF16),
                pltpu.VMEM((2, BLOCK_SLOTS, D_MODEL), BF16),
                pltpu.VMEM((CHUNK, D_MODEL), BF16),
                pltpu.SemaphoreType.DMA((2,)),
                pltpu.SemaphoreType.DMA(()),
            ],
        ),
        out_shape=jax.ShapeDtypeStruct((total_rows, D_MODEL), BF16),
        compiler_params=pltpu.CompilerParams(
            dimension_semantics=("arbitrary",), vmem_limit_bytes=VMEM_LIMIT),
        name="moe_dispatch",
    )(chunk_expert, chunk_dst, n_chunks, pad_dst, n_pad, hn, gate_t, earlier_t, upto)


def _expert_kernel(tile_expert_ref, n_tiles_ref, x_ref, w1_ref, w3_ref, w2_ref, y_ref):
    @pl.when(pl.program_id(0) < n_tiles_ref[0])
    def _():
        x = x_ref[...]
        a = _dot(x, w1_ref[0].astype(BF16))
        b = _dot(x, w3_ref[0].astype(BF16))
        act = (a * jax.nn.sigmoid(a) * b).astype(BF16)
        y_ref[...] = _dot(act, w2_ref[0].astype(BF16)).astype(BF16)


def _experts(tile_expert, n_tiles, xs, w1, w3, w2):
    rows = lambda m, te, nt: (jnp.minimum(m, nt[0] - 1), 0)
    weight = lambda m, te, nt: (te[m], 0, 0)
    return pl.pallas_call(
        _expert_kernel,
        grid_spec=pltpu.PrefetchScalarGridSpec(
            num_scalar_prefetch=2,
            grid=(xs.shape[0] // EXPERT_ROWS,),
            in_specs=[
                pl.BlockSpec((EXPERT_ROWS, D_MODEL), rows),
                pl.BlockSpec((1, D_MODEL, D_EXPERT), weight),
                pl.BlockSpec((1, D_MODEL, D_EXPERT), weight),
                pl.BlockSpec((1, D_EXPERT, D_MODEL), weight),
            ],
            out_specs=pl.BlockSpec((EXPERT_ROWS, D_MODEL), rows),
        ),
        out_shape=jax.ShapeDtypeStruct(xs.shape, BF16),
        compiler_params=pltpu.CompilerParams(
            dimension_semantics=("arbitrary",), vmem_limit_bytes=VMEM_LIMIT),
        name="moe_experts",
    )(tile_expert, n_tiles, xs, w1, w3, w2)


def _combine_kernel(chunk_expert_ref, chunk_dst_ref, n_chunks_ref,
                    gate_t_ref, earlier_ref, upto_ref, ys_ref, base_ref, o_ref,
                    key_ref, p_ref, rows_ref, sems):
    b = pl.program_id(0)
    last = pl.num_programs(0) - 1
    cur = b % 2
    n_chunks = n_chunks_ref[b]

    def fetch(blk, slot):
        _for_each(n_chunks_ref[blk], lambda c: _rows_copy(
            rows_ref.at[slot], ys_ref, c, chunk_dst_ref[blk * BLOCK_CHUNKS + c], 1,
            sems.at[slot], False).start())

    @pl.when(b == 0)
    def _():
        rows_ref[...] = jnp.zeros_like(rows_ref)
        fetch(0, 0)

    @pl.when(b < last)
    def _():
        fetch(jnp.minimum(b + 1, last), 1 - cur)

    group = MOE_BLOCK
    group_chunks = group // CHUNK
    _slot_keys(gate_t_ref, earlier_ref, upto_ref, key_ref)
    _fill_slot_rows(p_ref, key_ref, chunk_expert_ref, b * BLOCK_CHUNKS, n_chunks, gate_t_ref)

    def clear(c):
        rows = pl.ds(pl.multiple_of((n_chunks + c) * CHUNK, CHUNK), CHUNK)
        p_ref[rows, :] = jnp.zeros((CHUNK, MOE_BLOCK), p_ref.dtype)

    _for_each((group_chunks - n_chunks % group_chunks) % group_chunks, clear)
    _wait_chunks(rows_ref.at[cur], ys_ref, n_chunks, sems.at[cur], False)

    o_ref[...] = base_ref[...]

    def gather(g, carry):
        rows = pl.ds(pl.multiple_of(g * group, group), group)
        weights = p_ref[rows, :].T.astype(BF16)
        o_ref[...] += _dot(weights, rows_ref[cur, rows, :])
        return carry

    lax.fori_loop(0, (n_chunks * CHUNK + group - 1) // group, gather, 0)


def _combine(chunk_expert, chunk_dst, n_chunks, gate_t, earlier_t, upto, ys, base):
    n = base.shape[0]
    fixed = lambda i, *_: (0, 0)
    row = lambda i, *_: (i, 0)
    return pl.pallas_call(
        _combine_kernel,
        grid_spec=pltpu.PrefetchScalarGridSpec(
            num_scalar_prefetch=3,
            grid=(n // MOE_BLOCK,),
            in_specs=[
                pl.BlockSpec((N_EXPERTS, MOE_BLOCK), lambda i, *_: (0, i)),
                pl.BlockSpec(earlier_t.shape, fixed),
                pl.BlockSpec(upto.shape, fixed),
                pl.BlockSpec(memory_space=pl.ANY),
                pl.BlockSpec((MOE_BLOCK, D_MODEL), row),
            ],
            out_specs=pl.BlockSpec((MOE_BLOCK, D_MODEL), row),
            scratch_shapes=[
                pltpu.VMEM((N_EXPERTS, MOE_BLOCK), F32),
                pltpu.VMEM((BLOCK_SLOTS, MOE_BLOCK), F32),
                pltpu.VMEM((2, BLOCK_SLOTS, D_MODEL), BF16),
                pltpu.SemaphoreType.DMA((2,)),
            ],
        ),
        out_shape=jax.ShapeDtypeStruct((n, D_MODEL), F32),
        compiler_params=pltpu.CompilerParams(
            dimension_semantics=("arbitrary",), vmem_limit_bytes=VMEM_LIMIT),
        name="moe_combine",
    )(chunk_expert, chunk_dst, n_chunks, gate_t, earlier_t, upto, ys, base)


def _moe_layout(counts):
    nb = counts.shape[0]
    chunks = (counts + (CHUNK - 1)) // CHUNK
    first = jnp.cumsum(chunks, axis=1) - chunks
    n_chunks = jnp.sum(chunks, axis=1)
    tile_chunks = EXPERT_ROWS // CHUNK
    expert_tiles = (jnp.sum(chunks, axis=0) + tile_chunks - 1) // tile_chunks
    expert_first = (jnp.cumsum(expert_tiles) - expert_tiles) * tile_chunks
    dst = expert_first[None, :] + jnp.cumsum(chunks, axis=0) - chunks
    c = jnp.arange(BLOCK_CHUNKS)
    last = first + chunks
    chunk_expert = jnp.minimum(jnp.sum(c[None, None, :] >= last[:, :, None], axis=1),
                               N_EXPERTS - 1)
    owner = chunk_expert[:, None, :] == jnp.arange(N_EXPERTS)[None, :, None]
    pick = lambda a: jnp.sum(jnp.where(owner, a[:, :, None], 0), axis=1)
    chunk_dst = pick(dst - first) + c[None, :]
    max_tiles = nb * BLOCK_CHUNKS // tile_chunks + N_EXPERTS
    tile_expert = jnp.minimum(
        jnp.sum(jnp.arange(max_tiles)[:, None] >= jnp.cumsum(expert_tiles)[None, :], axis=1),
        N_EXPERTS - 1)
    expert_chunks = jnp.sum(chunks, axis=0)
    pad = expert_tiles * tile_chunks - expert_chunks
    j = jnp.arange(N_EXPERTS * tile_chunks)
    pad_expert = jnp.minimum(jnp.sum(j[None, :] >= jnp.cumsum(pad)[:, None], axis=0), N_EXPERTS - 1)
    pad_owner = pad_expert[None, :] == jnp.arange(N_EXPERTS)[:, None]
    pad_start = expert_first + expert_chunks - (jnp.cumsum(pad) - pad)
    pad_dst = jnp.sum(jnp.where(pad_owner, pad_start[:, None], 0), axis=0) + j
    i32 = lambda a: a.astype(jnp.int32)
    return (i32(chunk_expert).reshape(-1), i32(chunk_dst).reshape(-1), i32(n_chunks),
            i32(pad_dst), i32(jnp.sum(pad)).reshape(1),
            i32(tile_expert), i32(jnp.sum(expert_tiles)).reshape(1), max_tiles * EXPERT_ROWS)


def _layer(h, g_mix, w_in, lam_re, lam_im, b_re, b_im, c_re, c_im, d_skip, log_step,
           w_glu, b_glu, g_q, g_k, g_out_ssm, g_out_att, w_out, g_ffn, w_router,
           router_bias, e_w1, e_w3, e_w2, s_w1, s_w3, s_w2):
    bsz, seq, d = h.shape
    n = bsz * seq
    x2 = h.reshape(n, d)
    n_heads = D_ATT // HEAD_DIM

    head_id = jnp.arange(D_ATT) // HEAD_DIM
    hsum = jnp.where(head_id[:, None] == head_id[None, :], 1.0 / HEAD_DIM, 0.0).astype(BF16)
    ids = jnp.arange(ATT_TILE)
    later = (ids[:, None] > ids[None, :]).astype(BF16)
    zero, one = jnp.zeros_like(later), jnp.ones_like(later)
    tri = jnp.block([[later, zero, one, zero], [zero, later, zero, one]])

    gq = (jnp.tile(g_q, n_heads) * (math.log2(math.e) / math.sqrt(HEAD_DIM))).reshape(1, D_ATT)
    gk = jnp.tile(g_k, n_heads).reshape(1, D_ATT)
    u, q, k, v = _in_proj(x2, g_mix.reshape(1, d), w_in.astype(BF16), gq, gk, hsum)

    assert bsz == SUBLANES, "the S5 scan keeps one batch entry per sublane"
    s5w = _s5_weights(lam_re, lam_im, b_re, b_im, c_re, c_im, log_step)
    ssm = _s5(u.reshape(bsz, seq, D_SSM), *s5w, d_skip.reshape(1, D_SSM), w_glu.astype(BF16),
              b_glu.reshape(1, D_SSM), g_out_ssm.reshape(1, D_SSM)).reshape(n, D_SSM)

    att = _attention(q.reshape(bsz, seq, D_ATT), k.reshape(bsz, seq, D_ATT),
                     v.reshape(bsz, seq, D_ATT), tri).reshape(n, D_ATT)

    wr_t = w_router.T
    wr_hi = wr_t.astype(BF16)
    wr_lo = (wr_t - wr_hi.astype(F32)).astype(BF16)
    w_out_b = w_out.astype(BF16)
    base, hn, gate_t, counts = _post_mix(
        x2, ssm, att, g_out_att.reshape(1, D_ATT), w_out_b[:D_SSM], w_out_b[D_SSM:],
        g_ffn.reshape(1, d), wr_hi, wr_lo, router_bias.reshape(N_EXPERTS, 1),
        s_w1.astype(BF16), s_w3.astype(BF16), s_w2.astype(BF16))

    w1, w3, w2 = e_w1, e_w3, e_w2
    tok = jnp.arange(MOE_BLOCK)
    earlier_t = (tok[:, None] < tok[None, :]).astype(BF16)
    exp = jnp.arange(N_EXPERTS)
    upto = (exp[None, :] < exp[:, None]).astype(BF16)
    blocks_per_tile = ROW_TILE // MOE_BLOCK
    counts = jnp.round(counts[:, :, :blocks_per_tile]).astype(jnp.int32)
    counts = counts.transpose(0, 2, 1).reshape(n // MOE_BLOCK, N_EXPERTS)
    (chunk_expert, chunk_dst, n_chunks, pad_dst, n_pad, tile_expert, n_tiles,
     total_rows) = _moe_layout(counts)
    xs = _dispatch(chunk_expert, chunk_dst, n_chunks, pad_dst, n_pad, hn, gate_t, earlier_t, upto,
                   total_rows)
    ys = _experts(tile_expert, n_tiles, xs, w1, w3, w2)
    out = _combine(chunk_expert, chunk_dst, n_chunks, gate_t, earlier_t, upto, ys, base)
    return out.reshape(bsz, seq, d)


def kernel(x, g_mix, w_in, lam_re, lam_im, b_re, b_im, c_re, c_im, d_skip, log_step,
           w_glu, b_glu, g_q, g_k, g_out_ssm, g_out_att, w_out, g_ffn, w_router,
           router_bias, e_w1, e_w3, e_w2, s_w1, s_w3, s_w2):
    h = x
    for l in range(g_mix.shape[0]):
        h = _layer(h, g_mix[l], w_in[l], lam_re[l], lam_im[l], b_re[l], b_im[l], c_re[l],
                   c_im[l], d_skip[l], log_step[l], w_glu[l], b_glu[l], g_q[l], g_k[l],
                   g_out_ssm[l], g_out_att[l], w_out[l], g_ffn[l], w_router[l],
                   router_bias[l], e_w1[l], e_w3[l], e_w2[l], s_w1[l], s_w3[l], s_w2[l])
    return h
```

```python
import functools
import math

import jax
import jax.numpy as jnp
from jax import lax
from jax.experimental import pallas as pl
from jax.experimental.pallas import tpu as pltpu

F32 = jnp.float32
BF16 = jnp.bfloat16

D_MODEL = 1024
D_SSM = 512
SSM_GROUP = 16
N_SSM_GROUPS = 32
SSM_STATE = 64
D_ATT = 512
HEAD_DIM = 64
N_EXPERTS = 64
TOP_K = 8
N_EXPERT_GROUPS = 8
TOPK_GROUPS = 4
D_EXPERT = 256
ROUTED_SCALE = 2.5
EPS = 1e-6
LAMBDA_RE_MAX = -1e-4

LANES = 128
SUBLANES = 8
VMEM_LIMIT = 56 * 1024 * 1024

N_STATE = N_SSM_GROUPS * SSM_STATE
GROUPS_PER_TILE = LANES // SSM_GROUP
N_LANE_TILES = D_SSM // LANES
STATE_PER_TILE = GROUPS_PER_TILE * SSM_STATE

ROW_TILE = 512
S5_CHUNK = 64
S5_COLS = 1024
ATT_TILE = LANES
EXP2_UNDERFLOW = 174.0
ATT_LOOKBACK = 2
MOE_BLOCK = 256
CHUNK = 16
BLOCK_CHUNKS = MOE_BLOCK * TOP_K // CHUNK + N_EXPERTS
BLOCK_SLOTS = BLOCK_CHUNKS * CHUNK
PAIRS_PER_BLOCK = BLOCK_CHUNKS // 2
DISPATCH_ROWS = 1024
EXPERT_ROWS = 1024
LOOP_UNROLL = 4

_NT = (((1,), (1,)), ((), ()))


def _rms(x, g):
    ms = jnp.mean(x * x, axis=-1, keepdims=True)
    return x * lax.rsqrt(ms + EPS) * g


def _dot(a, b):
    return jnp.dot(a, b, preferred_element_type=F32)


def _in_proj_kernel(x_ref, g_ref, w_ref, gq_ref, gk_ref, hsum_ref,
                    u_ref, q_ref, k_ref, v_ref):
    hn = _rms(x_ref[...], g_ref[...]).astype(BF16)
    u_ref[...] = _dot(hn, w_ref[:, 0:D_SSM]).astype(BF16)

    def head_norm(lo, g):
        y = _dot(hn, w_ref[:, lo:lo + D_ATT])
        ms = _dot((y * y).astype(BF16), hsum_ref[...])
        return (y * lax.rsqrt(ms + EPS) * g).astype(BF16)

    q_ref[...] = head_norm(D_SSM, gq_ref[...])
    k_ref[...] = head_norm(D_SSM + D_ATT, gk_ref[...])
    v_ref[...] = _dot(hn, w_ref[:, D_SSM + 2 * D_ATT:]).astype(BF16)


def _in_proj(x2, g_mix, w_in, gq, gk, hsum):
    n = x2.shape[0]
    row = lambda i: (i, 0)
    fixed = lambda i: (0, 0)
    out = jax.ShapeDtypeStruct((n, D_SSM), BF16)
    return pl.pallas_call(
        _in_proj_kernel,
        grid=(n // ROW_TILE,),
        in_specs=[
            pl.BlockSpec((ROW_TILE, D_MODEL), row),
            pl.BlockSpec((1, D_MODEL), fixed),
            pl.BlockSpec(w_in.shape, fixed),
            pl.BlockSpec((1, D_ATT), fixed),
            pl.BlockSpec((1, D_ATT), fixed),
            pl.BlockSpec((D_ATT, D_ATT), fixed),
        ],
        out_specs=[pl.BlockSpec((ROW_TILE, D_SSM), row)] * 4,
        out_shape=[out] * 4,
        compiler_params=pltpu.CompilerParams(
            dimension_semantics=("parallel",), vmem_limit_bytes=VMEM_LIMIT),
        name="in_proj",
    )(x2, g_mix, w_in, gq, gk, hsum)


def _gelu_tanh(x):
    c = math.sqrt(2.0 / math.pi)
    return 0.5 * x * (1.0 + jnp.tanh(c * (x + 0.044715 * (x * x * x))))


def _s5_kernel(u_ref, bre_ref, bim_ref, lr_ref, li_ref, cre_ref, cimn_ref,
               d_ref, wg_ref, bg_ref, go_ref, o_ref, sre, sim, hre, him, tb):
    @pl.when(pl.program_id(0) == 0)
    def _():
        hre[...] = jnp.zeros_like(hre)
        him[...] = jnp.zeros_like(him)

    n_batch = u_ref.shape[0]
    tiles = [slice(a * LANES, (a + 1) * LANES) for a in range(N_LANE_TILES)]
    for b in range(n_batch):
        ub = u_ref[b].astype(F32)
        for a, lanes in enumerate(tiles):
            tb[a, pl.ds(b, S5_CHUNK, stride=n_batch), :] = ub[:, lanes]
    u_f32 = jnp.concatenate([tb[a] for a in range(N_LANE_TILES)], axis=1)
    u = u_f32.astype(BF16)
    for a in range(N_LANE_TILES):
        ua = u[:, a * LANES:(a + 1) * LANES]
        cols = slice(a * STATE_PER_TILE, (a + 1) * STATE_PER_TILE)
        sre[:, cols] = _dot(ua, bre_ref[a])
        sim[:, cols] = _dot(ua, bim_ref[a])

    for cb in range(N_STATE // S5_COLS):
        cols = slice(cb * S5_COLS, (cb + 1) * S5_COLS)
        lr = jnp.broadcast_to(lr_ref[:, cols], (SUBLANES, S5_COLS))
        li = jnp.broadcast_to(li_ref[:, cols], (SUBLANES, S5_COLS))

        def step(t, h, cols=cols, lr=lr, li=li):
            hr, hi = h
            rows = pl.ds(pl.multiple_of(t * SUBLANES, SUBLANES), SUBLANES)
            nr = lr * hr - li * hi + sre[rows, cols]
            ni = lr * hi + li * hr + sim[rows, cols]
            sre[rows, cols] = nr
            sim[rows, cols] = ni
            return nr, ni

        hr, hi = lax.fori_loop(0, S5_CHUNK, step, (hre[:, cols], him[:, cols]),
                               unroll=2)
        hre[:, cols] = hr
        him[:, cols] = hi

    ys = []
    for a in range(N_LANE_TILES):
        cols = slice(a * STATE_PER_TILE, (a + 1) * STATE_PER_TILE)
        ys.append(_dot(sre[:, cols].astype(BF16), cre_ref[a])
                  + _dot(sim[:, cols].astype(BF16), cimn_ref[a]))
    y = jnp.concatenate(ys, axis=1)
    y = _gelu_tanh(y + d_ref[...] * u_f32)
    y = y * jax.nn.sigmoid(_dot(y.astype(BF16), wg_ref[...]) + bg_ref[...])
    y = _rms(y, go_ref[...])
    for a, lanes in enumerate(tiles):
        tb[a] = y[:, lanes]
    for b in range(n_batch):
        o_ref[b] = jnp.concatenate(
            [tb[a, pl.ds(b, S5_CHUNK, stride=n_batch), :] for a in range(N_LANE_TILES)],
            axis=1).astype(BF16)


def _s5(u, bre, bim, lr, li, cre, cimn, d_skip, w_glu, b_glu, g_out):
    bsz, seq, _ = u.shape
    rows = S5_CHUNK * bsz
    fixed2 = lambda i: (0, 0)
    fixed3 = lambda i: (0, 0, 0)
    return pl.pallas_call(
        _s5_kernel,
        grid=(seq // S5_CHUNK,),
        in_specs=[
            pl.BlockSpec((bsz, S5_CHUNK, D_SSM), lambda i: (0, i, 0)),
            pl.BlockSpec(bre.shape, fixed3),
            pl.BlockSpec(bim.shape, fixed3),
            pl.BlockSpec((1, N_STATE), fixed2),
            pl.BlockSpec((1, N_STATE), fixed2),
            pl.BlockSpec(cre.shape, fixed3),
            pl.BlockSpec(cimn.shape, fixed3),
            pl.BlockSpec((1, D_SSM), fixed2),
            pl.BlockSpec((D_SSM, D_SSM), fixed2),
            pl.BlockSpec((1, D_SSM), fixed2),
            pl.BlockSpec((1, D_SSM), fixed2),
        ],
        out_specs=pl.BlockSpec((bsz, S5_CHUNK, D_SSM), lambda i: (0, i, 0)),
        out_shape=jax.ShapeDtypeStruct(u.shape, BF16),
        scratch_shapes=[
            pltpu.VMEM((rows, N_STATE), F32),
            pltpu.VMEM((rows, N_STATE), F32),
            pltpu.VMEM((SUBLANES, N_STATE), F32),
            pltpu.VMEM((SUBLANES, N_STATE), F32),
            pltpu.VMEM((N_LANE_TILES, rows, LANES), F32),
        ],
        compiler_params=pltpu.CompilerParams(
            dimension_semantics=("arbitrary",), vmem_limit_bytes=VMEM_LIMIT),
        name="s5_mixer",
    )(u, bre, bim, lr, li, cre, cimn, d_skip, w_glu, b_glu, g_out)


def _s5_weights(lam_re, lam_im, b_re, b_im, c_re, c_im, log_step):
    lre = jnp.minimum(lam_re, LAMBDA_RE_MAX)
    step = jnp.exp(log_step)[:, None]
    mag = jnp.exp(lre * step)
    bar_re = mag * jnp.cos(lam_im * step)
    bar_im = mag * jnp.sin(lam_im * step)
    den = lre * lre + lam_im * lam_im
    coef_re = ((bar_re - 1.0) * lre + bar_im * lam_im) / den
    coef_im = (bar_im * lre - (bar_re - 1.0) * lam_im) / den
    bbar_re = coef_re[..., None] * b_re - coef_im[..., None] * b_im
    bbar_im = coef_re[..., None] * b_im + coef_im[..., None] * b_re
    eye = jnp.eye(GROUPS_PER_TILE, dtype=F32)

    def pack_b(b):
        b = b.reshape(N_LANE_TILES, GROUPS_PER_TILE, SSM_STATE, SSM_GROUP)
        m = jnp.einsum('agph,gk->aghkp', b, eye)
        return m.reshape(N_LANE_TILES, LANES, STATE_PER_TILE).astype(BF16)

    def pack_c(c):
        c = c.reshape(N_LANE_TILES, GROUPS_PER_TILE, SSM_GROUP, SSM_STATE)
        m = jnp.einsum('aghp,gk->agpkh', c, eye)
        return m.reshape(N_LANE_TILES, STATE_PER_TILE, LANES).astype(BF16)

    return (pack_b(bbar_re), pack_b(bbar_im),
            bar_re.reshape(1, N_STATE), bar_im.reshape(1, N_STATE),
            pack_c(c_re), pack_c(-c_im))


def _attn_kernel(q_ref, k_ref, v_ref, tri_ref, o_ref, *scratch):
    acc_refs, car_refs = scratch[:D_ATT // LANES], scratch[D_ATT // LANES:]
    t = ATT_TILE
    i = pl.program_id(1)
    first_head = lax.broadcasted_iota(jnp.int32, (1, LANES), 1) < HEAD_DIM

    def split_heads(ref, rows, lanes):
        x = ref[0, rows, lanes]
        zero = jnp.zeros_like(x)
        return jnp.concatenate([jnp.where(first_head, x, zero),
                                jnp.where(first_head, zero, x)], axis=0)

    def visit(first, count, on_diagonal):
        tiles = [slice(p * LANES, (p + 1) * LANES) for p in range(D_ATT // LANES)]
        rows = [pl.ds(pl.multiple_of((first - b) * t, t), t) for b in range(count)]
        if on_diagonal:
            col = lax.broadcasted_iota(jnp.int32, (t, 2 * t), 1)
            row = lax.broadcasted_iota(jnp.int32, (t, 2 * t), 0)
            strictly_before = jnp.where(col >= t, col - t, col) < row
        masked = lambda b: on_diagonal and b == 0
        zs = [[lax.dot_general(q_ref[0, :, lanes], split_heads(k_ref, r, lanes),
                               _NT, preferred_element_type=F32) for lanes in tiles] for r in rows]
        sps = [[jnp.maximum(z, 0.0) + jnp.log2(1.0 + jnp.exp2(-jnp.abs(z))) for z in zb]
               for zb in zs]
        sps = [[jnp.where(strictly_before, sp, 0.0) if masked(b) else sp for sp in spb]
               for b, spb in enumerate(sps)]
        sums = [[_dot(sp.astype(BF16), tri_ref[...]) for sp in spb] for spb in sps]
        smallest = None
        for p, (lanes, acc_ref, car_ref) in enumerate(zip(tiles, acc_refs, car_refs)):
            car = car_ref[...]
            acc = acc_ref[...]
            for b, r in enumerate(rows):
                w = jnp.exp2(zs[b][p] - sps[b][p] - (sums[b][p][:, :2 * t] + car))
                if masked(b):
                    w = jnp.where(strictly_before, w, 0.0)
                acc = acc + _dot(w.astype(BF16), split_heads(v_ref, r, lanes))
                car = car + sums[b][p][:, 2 * t:]
            acc_ref[...] = acc
            car_ref[...] = car
            smallest = car if smallest is None else jnp.minimum(smallest, car)
        return jnp.min(smallest)

    for ref in scratch:
        ref[...] = jnp.zeros_like(ref)
    passes = [functools.partial(visit, i, c + 1, True) for c in range(ATT_LOOKBACK + 1)]
    carried = lax.switch(jnp.minimum(i, ATT_LOOKBACK), passes)

    def more(state):
        j, carried = state
        return (j >= 0) & (carried < EXP2_UNDERFLOW)

    def earlier(state):
        j, _ = state
        return j - 1, visit(j, 1, False)

    lax.while_loop(more, earlier, (i - 1 - ATT_LOOKBACK, carried))
    o_ref[0] = jnp.concatenate(
        [ref[...] for ref in acc_refs], axis=1).astype(o_ref.dtype)


def _attention(q, k, v, tri):
    bsz, seq, _ = q.shape
    t = ATT_TILE
    n_tiles = D_ATT // LANES
    return pl.pallas_call(
        _attn_kernel,
        grid=(bsz, seq // t),
        in_specs=[
            pl.BlockSpec((1, t, D_ATT), lambda b, i: (b, i, 0)),
            pl.BlockSpec((1, seq, D_ATT), lambda b, i: (b, 0, 0)),
            pl.BlockSpec((1, seq, D_ATT), lambda b, i: (b, 0, 0)),
            pl.BlockSpec(tri.shape, lambda b, i: (0, 0)),
        ],
        out_specs=pl.BlockSpec((1, t, D_ATT), lambda b, i: (b, i, 0)),
        out_shape=jax.ShapeDtypeStruct((bsz, seq, D_ATT), BF16),
        scratch_shapes=([pltpu.VMEM((t, LANES), F32)] * n_tiles
                        + [pltpu.VMEM((t, 2 * t), F32)] * n_tiles),
        compiler_params=pltpu.CompilerParams(
            dimension_semantics=("parallel", "parallel"),
            vmem_limit_bytes=VMEM_LIMIT),
        name="stick_breaking_attention",
    )(q, k, v, tri)


def _pick_first_max(vals, taken, ridx, n):
    cand = jnp.where(taken > 0, -jnp.inf, vals)
    m = jnp.max(cand, axis=0, keepdims=True)
    first = jnp.min(jnp.where((cand == m) & (taken == 0), ridx, n), axis=0, keepdims=True)
    return jnp.where(ridx == first, 1.0, taken)


def _route(scores, biased):
    tm = scores.shape[1]
    per_group = N_EXPERTS // N_EXPERT_GROUPS
    ridx8 = lax.broadcasted_iota(jnp.int32, (per_group, tm), 0)
    zeros8 = jnp.zeros((per_group, tm), F32)
    group_scores = []
    for g in range(N_EXPERT_GROUPS):
        blk = biased[g * per_group:(g + 1) * per_group]
        top1 = _pick_first_max(blk, zeros8, ridx8, per_group)
        top2 = _pick_first_max(blk, top1, ridx8, per_group)
        group_scores.append(jnp.sum(jnp.where(top2 > 0, blk, 0.0), axis=0, keepdims=True))
    grp = jnp.concatenate(group_scores, axis=0)
    ridx_g = lax.broadcasted_iota(jnp.int32, (N_EXPERT_GROUPS, tm), 0)
    grp_taken = jnp.zeros((N_EXPERT_GROUPS, tm), F32)
    for _ in range(TOPK_GROUPS):
        grp_taken = _pick_first_max(grp, grp_taken, ridx_g, N_EXPERT_GROUPS)
    expert_mask = jnp.concatenate(
        [jnp.broadcast_to(grp_taken[g:g + 1], (per_group, tm)) for g in range(N_EXPERT_GROUPS)],
        axis=0)
    sel = jnp.where(expert_mask > 0, biased, -jnp.inf)
    ridx = lax.broadcasted_iota(jnp.int32, (N_EXPERTS, tm), 0)
    taken = jnp.zeros((N_EXPERTS, tm), F32)
    for _ in range(TOP_K):
        taken = _pick_first_max(sel, taken, ridx, N_EXPERTS)
    w = jnp.where(taken > 0, scores, 0.0)
    return w / jnp.sum(w, axis=0, keepdims=True) * ROUTED_SCALE


def _post_mix_kernel(x_ref, ssm_ref, att_ref, ga_ref, wos_ref, woa_ref, gf_ref,
                     wrh_ref, wrl_ref, rb_ref, s1_ref, s3_ref, s2_ref,
                     base_ref, hn_ref, gate_t_ref, cnt_ref):
    att_n = _rms(att_ref[...].astype(F32), ga_ref[...]).astype(BF16)
    h = x_ref[...] + _dot(ssm_ref[...], wos_ref[...]) + _dot(att_n, woa_ref[...])
    hn = _rms(h, gf_ref[...])
    hn_hi = hn.astype(BF16)
    hn_lo = (hn - hn_hi.astype(F32)).astype(BF16)
    hn_ref[...] = hn_hi

    nt = functools.partial(lax.dot_general, dimension_numbers=_NT, preferred_element_type=F32)
    logits = (nt(wrh_ref[...], hn_hi) + nt(wrh_ref[...], hn_lo) + nt(wrl_ref[...], hn_hi))
    scores = jax.nn.sigmoid(logits)
    gate_t = _route(scores, scores + rb_ref[...])
    gate_t_ref[...] = gate_t
    lane = lax.broadcasted_iota(jnp.int32, (1, LANES), 1)
    routed = jnp.where(gate_t > 0.0, 1.0, 0.0)
    counts = jnp.zeros((N_EXPERTS, LANES), F32)
    for blk in range(ROW_TILE // MOE_BLOCK):
        c = jnp.sum(routed[:, blk * MOE_BLOCK:(blk + 1) * MOE_BLOCK], axis=1, keepdims=True)
        counts = jnp.where(lane == blk, c, counts)
    cnt_ref[0] = counts

    a = _dot(hn_hi, s1_ref[...])
    b = _dot(hn_hi, s3_ref[...])
    act = (a * jax.nn.sigmoid(a) * b).astype(BF16)
    base_ref[...] = h + _dot(act, s2_ref[...])


def _post_mix(x2, ssm, att, g_att, wo_s, wo_a, g_ffn, wr_hi, wr_lo, rbias, s1, s3, s2):
    n = x2.shape[0]
    row = lambda i: (i, 0)
    fixed = lambda i: (0, 0)
    full = lambda a: pl.BlockSpec(a.shape, fixed)
    return pl.pallas_call(
        _post_mix_kernel,
        grid=(n // ROW_TILE,),
        in_specs=[
            pl.BlockSpec((ROW_TILE, D_MODEL), row),
            pl.BlockSpec((ROW_TILE, D_SSM), row),
            pl.BlockSpec((ROW_TILE, D_ATT), row),
            full(g_att), full(wo_s), full(wo_a), full(g_ffn),
            full(wr_hi), full(wr_lo), full(rbias), full(s1), full(s3), full(s2),
        ],
        out_specs=[
            pl.BlockSpec((ROW_TILE, D_MODEL), row),
            pl.BlockSpec((ROW_TILE, D_MODEL), row),
            pl.BlockSpec((N_EXPERTS, ROW_TILE), lambda i: (0, i)),
            pl.BlockSpec((1, N_EXPERTS, LANES), lambda i: (i, 0, 0)),
        ],
        out_shape=[
            jax.ShapeDtypeStruct((n, D_MODEL), F32),
            jax.ShapeDtypeStruct((n, D_MODEL), BF16),
            jax.ShapeDtypeStruct((N_EXPERTS, n), F32),
            jax.ShapeDtypeStruct((n // ROW_TILE, N_EXPERTS, LANES), F32),
        ],
        compiler_params=pltpu.CompilerParams(
            dimension_semantics=("parallel",), vmem_limit_bytes=VMEM_LIMIT),
        name="post_mix",
    )(x2, ssm, att, g_att, wo_s, wo_a, g_ffn, wr_hi, wr_lo, rbias, s1, s3, s2)


def _slot_keys(gate_t_ref, earlier_ref, upto_ref, key_ref):
    sel = gate_t_ref[...] > 0.0
    sel_f = jnp.where(sel, 1.0, 0.0)
    rank = _dot(sel_f.astype(BF16), earlier_ref[...])
    count = jnp.sum(sel_f, axis=1, keepdims=True)
    padded = jnp.floor((count + (CHUNK - 1)) * (1.0 / CHUNK)) * CHUNK
    first = _dot(upto_ref[...], jnp.broadcast_to(padded, (N_EXPERTS, LANES)).astype(BF16))
    first = jnp.concatenate([first] * (MOE_BLOCK // LANES), axis=1)
    key_ref[...] = jnp.where(sel, rank + first, -1.0)


def _for_each(n, body):
    trips = n // LOOP_UNROLL

    def unrolled(i, carry):
        for u in range(LOOP_UNROLL):
            body(i * LOOP_UNROLL + u)
        return carry

    def single(i, carry):
        body(i)
        return carry

    lax.fori_loop(0, trips, unrolled, 0)
    lax.fori_loop(trips * LOOP_UNROLL, n, single, 0)


def _fill_slot_rows(p_ref, key_ref, chunk_expert_ref, base, n_chunks, value_ref=None):
    within = lax.broadcasted_iota(jnp.int32, (CHUNK, MOE_BLOCK), 0).astype(F32)

    def fill(c):
        e = chunk_expert_ref[base + c]
        hit = (key_ref[pl.ds(e, 1), :] - (c * CHUNK).astype(F32)) == within
        vals = 1.0 if value_ref is None else value_ref[pl.ds(e, 1), :]
        rows = pl.ds(pl.multiple_of(c * CHUNK, CHUNK), CHUNK)
        p_ref[rows, :] = jnp.where(hit, vals, 0.0).astype(p_ref.dtype)

    _for_each(n_chunks, fill)


def _rows_copy(block_rows_ref, global_ref, c, dst, n, sem, to_global):
    local = block_rows_ref.at[pl.ds(pl.multiple_of(c * CHUNK, CHUNK), n * CHUNK), :]
    remote = global_ref.at[pl.ds(pl.multiple_of(dst * CHUNK, CHUNK), n * CHUNK), :]
    return (pltpu.make_async_copy(local, remote, sem) if to_global
            else pltpu.make_async_copy(remote, local, sem))


def _wait_chunks(block_rows_ref, global_ref, n_chunks, sem, to_global):
    bit = 1 << (BLOCK_CHUNKS.bit_length() - 1)
    while bit:
        @pl.when((n_chunks & bit) != 0)
        def _(bit=bit):
            _rows_copy(block_rows_ref, global_ref, 0, 0, bit, sem, to_global).wait()
        bit >>= 1


def _start_copies(block_rows_ref, global_ref, pairs_ref, singles_ref, counts_ref, blk, sem,
                  to_global):
    for width, list_ref, per_block, which in ((2, pairs_ref, PAIRS_PER_BLOCK, 1),
                                              (1, singles_ref, N_EXPERTS, 2)):
        def start(i, width=width, list_ref=list_ref, per_block=per_block):
            packed = list_ref[blk * per_block + i]
            _rows_copy(block_rows_ref, global_ref, packed & 255, packed >> 8, width, sem,
                       to_global).start()

        _for_each(counts_ref[3 * blk + which], start)


def _dispatch_kernel(chunk_expert_ref, pairs_ref, singles_ref, counts_ref, pad_dst_ref, n_pad_ref,
                     hn_ref, gate_t_ref, earlier_ref, upto_ref, xs_ref,
                     key_ref, p_ref, rows_ref, zero_ref, sems, zero_sem):
    b = pl.program_id(0)
    last = pl.num_programs(0) - 1
    cur = b % 2
    base = b * BLOCK_CHUNKS
    n_chunks = counts_ref[3 * b]

    @pl.when(b == 0)
    def _():
        p_ref[...] = jnp.zeros_like(p_ref)
        zero_ref[...] = jnp.zeros_like(zero_ref)

        def pad_copy(j):
            dst = pl.ds(pl.multiple_of(pad_dst_ref[j] * CHUNK, CHUNK), CHUNK)
            return pltpu.make_async_copy(zero_ref, xs_ref.at[dst, :], zero_sem)

        _for_each(n_pad_ref[0], lambda j: pad_copy(j).start())
        _for_each(n_pad_ref[0], lambda j: pad_copy(j).wait())

    _slot_keys(gate_t_ref, earlier_ref, upto_ref, key_ref)
    _fill_slot_rows(p_ref, key_ref, chunk_expert_ref, base, n_chunks)

    group = DISPATCH_ROWS

    def permute(g, carry):
        rows = pl.ds(pl.multiple_of(g * group, group), group)
        rows_ref[cur, rows, :] = _dot(p_ref[rows, :], hn_ref[...]).astype(BF16)
        return carry

    lax.fori_loop(0, (n_chunks * CHUNK + group - 1) // group, permute, 0)

    _start_copies(rows_ref.at[cur], xs_ref, pairs_ref, singles_ref, counts_ref, b, sems.at[cur],
                  True)

    @pl.when(b > 0)
    def _():
        _wait_chunks(rows_ref.at[1 - cur], xs_ref, counts_ref[3 * jnp.maximum(b - 1, 0)],
                     sems.at[1 - cur], True)

    @pl.when(b == last)
    def _():
        _wait_chunks(rows_ref.at[cur], xs_ref, n_chunks, sems.at[cur], True)


def _dispatch(chunk_expert, pairs, singles, counts, pad_dst, n_pad, hn, gate_t, earlier_t, upto,
              total_rows):
    nb = hn.shape[0] // MOE_BLOCK
    fixed = lambda i, *_: (0, 0)
    return pl.pallas_call(
        _dispatch_kernel,
        grid_spec=pltpu.PrefetchScalarGridSpec(
            num_scalar_prefetch=6,
            grid=(nb,),
            in_specs=[
                pl.BlockSpec((MOE_BLOCK, D_MODEL), lambda i, *_: (i, 0)),
                pl.BlockSpec((N_EXPERTS, MOE_BLOCK), lambda i, *_: (0, i)),
                pl.BlockSpec(earlier_t.shape, fixed),
                pl.BlockSpec(upto.shape, fixed),
            ],
            out_specs=pl.BlockSpec(memory_space=pl.ANY),
            scratch_shapes=[
                pltpu.VMEM((N_EXPERTS, MOE_BLOCK), F32),
                pltpu.VMEM((BLOCK_SLOTS, MOE_BLOCK), BF16),
                pltpu.VMEM((2, BLOCK_SLOTS, D_MODEL), BF16),
                pltpu.VMEM((CHUNK, D_MODEL), BF16),
                pltpu.SemaphoreType.DMA((2,)),
                pltpu.SemaphoreType.DMA(()),
            ],
        ),
        out_shape=jax.ShapeDtypeStruct((total_rows, D_MODEL), BF16),
        compiler_params=pltpu.CompilerParams(
            dimension_semantics=("arbitrary",), vmem_limit_bytes=VMEM_LIMIT),
        name="moe_dispatch",
    )(chunk_expert, pairs, singles, counts, pad_dst, n_pad, hn, gate_t, earlier_t, upto)


def _expert_kernel(tile_expert_ref, n_tiles_ref, x_ref, w1_ref, w3_ref, w2_ref, y_ref):
    @pl.when(pl.program_id(0) < n_tiles_ref[0])
    def _():
        x = x_ref[...]
        a = _dot(x, w1_ref[0].astype(BF16))
        b = _dot(x, w3_ref[0].astype(BF16))
        act = (a * jax.nn.sigmoid(a) * b).astype(BF16)
        y_ref[...] = _dot(act, w2_ref[0].astype(BF16)).astype(BF16)


def _experts(tile_expert, n_tiles, xs, w1, w3, w2):
    rows = lambda m, te, nt: (jnp.minimum(m, nt[0] - 1), 0)
    weight = lambda m, te, nt: (te[m], 0, 0)
    return pl.pallas_call(
        _expert_kernel,
        grid_spec=pltpu.PrefetchScalarGridSpec(
            num_scalar_prefetch=2,
            grid=(xs.shape[0] // EXPERT_ROWS,),
            in_specs=[
                pl.BlockSpec((EXPERT_ROWS, D_MODEL), rows),
                pl.BlockSpec((1, D_MODEL, D_EXPERT), weight),
                pl.BlockSpec((1, D_MODEL, D_EXPERT), weight),
                pl.BlockSpec((1, D_EXPERT, D_MODEL), weight),
            ],
            out_specs=pl.BlockSpec((EXPERT_ROWS, D_MODEL), rows),
        ),
        out_shape=jax.ShapeDtypeStruct(xs.shape, BF16),
        compiler_params=pltpu.CompilerParams(
            dimension_semantics=("arbitrary",), vmem_limit_bytes=VMEM_LIMIT),
        name="moe_experts",
    )(tile_expert, n_tiles, xs, w1, w3, w2)


def _combine_kernel(chunk_expert_ref, pairs_ref, singles_ref, counts_ref,
                    gate_t_ref, earlier_ref, upto_ref, ys_ref, base_ref, o_ref,
                    key_ref, p_ref, rows_ref, sems):
    b = pl.program_id(0)
    last = pl.num_programs(0) - 1
    cur = b % 2
    n_chunks = counts_ref[3 * b]

    def fetch(blk, slot):
        _start_copies(rows_ref.at[slot], ys_ref, pairs_ref, singles_ref, counts_ref, blk,
                      sems.at[slot], False)

    @pl.when(b == 0)
    def _():
        rows_ref[...] = jnp.zeros_like(rows_ref)
        fetch(0, 0)

    @pl.when(b < last)
    def _():
        fetch(jnp.minimum(b + 1, last), 1 - cur)

    group = MOE_BLOCK
    group_chunks = group // CHUNK
    _slot_keys(gate_t_ref, earlier_ref, upto_ref, key_ref)
    _fill_slot_rows(p_ref, key_ref, chunk_expert_ref, b * BLOCK_CHUNKS, n_chunks, gate_t_ref)

    def clear(c):
        rows = pl.ds(pl.multiple_of((n_chunks + c) * CHUNK, CHUNK), CHUNK)
        p_ref[rows, :] = jnp.zeros((CHUNK, MOE_BLOCK), p_ref.dtype)

    _for_each((group_chunks - n_chunks % group_chunks) % group_chunks, clear)
    _wait_chunks(rows_ref.at[cur], ys_ref, n_chunks, sems.at[cur], False)

    o_ref[...] = base_ref[...]

    def gather(g, carry):
        rows = pl.ds(pl.multiple_of(g * group, group), group)
        o_ref[...] += lax.dot_general(p_ref[rows, :].astype(BF16), rows_ref[cur, rows, :],
                                      (((0,), (0,)), ((), ())), preferred_element_type=F32)
        return carry

    lax.fori_loop(0, (n_chunks * CHUNK + group - 1) // group, gather, 0)


def _combine(chunk_expert, pairs, singles, counts, gate_t, earlier_t, upto, ys, base):
    n = base.shape[0]
    fixed = lambda i, *_: (0, 0)
    row = lambda i, *_: (i, 0)
    return pl.pallas_call(
        _combine_kernel,
        grid_spec=pltpu.PrefetchScalarGridSpec(
            num_scalar_prefetch=4,
            grid=(n // MOE_BLOCK,),
            in_specs=[
                pl.BlockSpec((N_EXPERTS, MOE_BLOCK), lambda i, *_: (0, i)),
                pl.BlockSpec(earlier_t.shape, fixed),
                pl.BlockSpec(upto.shape, fixed),
                pl.BlockSpec(memory_space=pl.ANY),
                pl.BlockSpec((MOE_BLOCK, D_MODEL), row),
            ],
            out_specs=pl.BlockSpec((MOE_BLOCK, D_MODEL), row),
            scratch_shapes=[
                pltpu.VMEM((N_EXPERTS, MOE_BLOCK), F32),
                pltpu.VMEM((BLOCK_SLOTS, MOE_BLOCK), F32),
                pltpu.VMEM((2, BLOCK_SLOTS, D_MODEL), BF16),
                pltpu.SemaphoreType.DMA((2,)),
            ],
        ),
        out_shape=jax.ShapeDtypeStruct((n, D_MODEL), F32),
        compiler_params=pltpu.CompilerParams(
            dimension_semantics=("arbitrary",), vmem_limit_bytes=VMEM_LIMIT),
        name="moe_combine",
    )(chunk_expert, pairs, singles, counts, gate_t, earlier_t, upto, ys, base)


def _moe_layout(counts):
    nb = counts.shape[0]
    chunks = (counts + (CHUNK - 1)) // CHUNK
    first = jnp.cumsum(chunks, axis=1) - chunks
    n_chunks = jnp.sum(chunks, axis=1)
    tile_chunks = EXPERT_ROWS // CHUNK
    expert_tiles = (jnp.sum(chunks, axis=0) + tile_chunks - 1) // tile_chunks
    expert_first = (jnp.cumsum(expert_tiles) - expert_tiles) * tile_chunks
    dst = expert_first[None, :] + jnp.cumsum(chunks, axis=0) - chunks
    c = jnp.arange(BLOCK_CHUNKS)
    last = first + chunks
    chunk_expert = jnp.minimum(jnp.sum(c[None, None, :] >= last[:, :, None], axis=1),
                               N_EXPERTS - 1)

    def copy_list(per_run, offset, width, length):
        ends = jnp.cumsum(per_run, axis=1)
        i = jnp.arange(length)
        run = jnp.minimum(jnp.sum(i[None, None, :] >= ends[:, :, None], axis=1), N_EXPERTS - 1)
        owner = run[:, None, :] == jnp.arange(N_EXPERTS)[None, :, None]
        start = offset - width * (ends - per_run)
        packed = (dst + start) * 256 + (first + start)
        return (jnp.sum(jnp.where(owner, packed[:, :, None], 0), axis=1)
                + 257 * width * i[None, :])

    pairs = copy_list(chunks // 2, jnp.zeros_like(chunks), 2, PAIRS_PER_BLOCK)
    singles = copy_list(chunks % 2, chunks - chunks % 2, 1, N_EXPERTS)
    counts3 = jnp.stack([n_chunks, jnp.sum(chunks // 2, axis=1), jnp.sum(chunks % 2, axis=1)],
                        axis=1)
    max_tiles = nb * BLOCK_CHUNKS // tile_chunks + N_EXPERTS
    tile_expert = jnp.minimum(
        jnp.sum(jnp.arange(max_tiles)[:, None] >= jnp.cumsum(expert_tiles)[None, :], axis=1),
        N_EXPERTS - 1)
    expert_chunks = jnp.sum(chunks, axis=0)
    pad = expert_tiles * tile_chunks - expert_chunks
    j = jnp.arange(N_EXPERTS * tile_chunks)
    pad_expert = jnp.minimum(jnp.sum(j[None, :] >= jnp.cumsum(pad)[:, None], axis=0), N_EXPERTS - 1)
    pad_owner = pad_expert[None, :] == jnp.arange(N_EXPERTS)[:, None]
    pad_start = expert_first + expert_chunks - (jnp.cumsum(pad) - pad)
    pad_dst = jnp.sum(jnp.where(pad_owner, pad_start[:, None], 0), axis=0) + j
    i32 = lambda a: a.astype(jnp.int32)
    return (i32(chunk_expert).reshape(-1), i32(pairs).reshape(-1), i32(singles).reshape(-1),
            i32(counts3).reshape(-1), i32(pad_dst), i32(jnp.sum(pad)).reshape(1),
            i32(tile_expert), i32(jnp.sum(expert_tiles)).reshape(1), max_tiles * EXPERT_ROWS)


def _layer(h, g_mix, w_in, lam_re, lam_im, b_re, b_im, c_re, c_im, d_skip, log_step,
           w_glu, b_glu, g_q, g_k, g_out_ssm, g_out_att, w_out, g_ffn, w_router,
           router_bias, e_w1, e_w3, e_w2, s_w1, s_w3, s_w2):
    bsz, seq, d = h.shape
    n = bsz * seq
    x2 = h.reshape(n, d)
    n_heads = D_ATT // HEAD_DIM

    head_id = jnp.arange(D_ATT) // HEAD_DIM
    hsum = jnp.where(head_id[:, None] == head_id[None, :], 1.0 / HEAD_DIM, 0.0).astype(BF16)
    ids = jnp.arange(ATT_TILE)
    later = (ids[:, None] > ids[None, :]).astype(BF16)
    zero, one = jnp.zeros_like(later), jnp.ones_like(later)
    tri = jnp.block([[later, zero, one, zero], [zero, later, zero, one]])

    gq = (jnp.tile(g_q, n_heads) * (math.log2(math.e) / math.sqrt(HEAD_DIM))).reshape(1, D_ATT)
    gk = jnp.tile(g_k, n_heads).reshape(1, D_ATT)
    u, q, k, v = _in_proj(x2, g_mix.reshape(1, d), w_in.astype(BF16), gq, gk, hsum)

    assert bsz == SUBLANES, "the S5 scan keeps one batch entry per sublane"
    s5w = _s5_weights(lam_re, lam_im, b_re, b_im, c_re, c_im, log_step)
    ssm = _s5(u.reshape(bsz, seq, D_SSM), *s5w, d_skip.reshape(1, D_SSM), w_glu.astype(BF16),
              b_glu.reshape(1, D_SSM), g_out_ssm.reshape(1, D_SSM)).reshape(n, D_SSM)

    att = _attention(q.reshape(bsz, seq, D_ATT), k.reshape(bsz, seq, D_ATT),
                     v.reshape(bsz, seq, D_ATT), tri).reshape(n, D_ATT)

    wr_t = w_router.T
    wr_hi = wr_t.astype(BF16)
    wr_lo = (wr_t - wr_hi.astype(F32)).astype(BF16)
    w_out_b = w_out.astype(BF16)
    base, hn, gate_t, counts = _post_mix(
        x2, ssm, att, g_out_att.reshape(1, D_ATT), w_out_b[:D_SSM], w_out_b[D_SSM:],
        g_ffn.reshape(1, d), wr_hi, wr_lo, router_bias.reshape(N_EXPERTS, 1),
        s_w1.astype(BF16), s_w3.astype(BF16), s_w2.astype(BF16))

    w1, w3, w2 = e_w1, e_w3, e_w2
    tok = jnp.arange(MOE_BLOCK)
    earlier_t = (tok[:, None] < tok[None, :]).astype(BF16)
    exp = jnp.arange(N_EXPERTS)
    upto = (exp[None, :] < exp[:, None]).astype(BF16)
    blocks_per_tile = ROW_TILE // MOE_BLOCK
    counts = jnp.round(counts[:, :, :blocks_per_tile]).astype(jnp.int32)
    counts = counts.transpose(0, 2, 1).reshape(n // MOE_BLOCK, N_EXPERTS)
    (chunk_expert, pairs, singles, copy_counts, pad_dst, n_pad, tile_expert, n_tiles,
     total_rows) = _moe_layout(counts)
    xs = _dispatch(chunk_expert, pairs, singles, copy_counts, pad_dst, n_pad, hn, gate_t,
                   earlier_t, upto, total_rows)
    ys = _experts(tile_expert, n_tiles, xs, w1, w3, w2)
    out = _combine(chunk_expert, pairs, singles, copy_counts, gate_t, earlier_t, upto, ys, base)
    return out.reshape(bsz, seq, d)


def kernel(x, g_mix, w_in, lam_re, lam_im, b_re, b_im, c_re, c_im, d_skip, log_step,
           w_glu, b_glu, g_q, g_k, g_out_ssm, g_out_att, w_out, g_ffn, w_router,
           router_bias, e_w1, e_w3, e_w2, s_w1, s_w3, s_w2):
    h = x
    for l in range(g_mix.shape[0]):
        h = _layer(h, g_mix[l], w_in[l], lam_re[l], lam_im[l], b_re[l], b_im[l], c_re[l],
                   c_im[l], d_skip[l], log_step[l], w_glu[l], b_glu[l], g_q[l], g_k[l],
                   g_out_ssm[l], g_out_att[l], w_out[l], g_ffn[l], w_router[l],
                   router_bias[l], e_w1[l], e_w3[l], e_w2[l], s_w1[l], s_w3[l], s_w2[l])
    return h
```

```python
import functools
import math

import jax
import jax.numpy as jnp
from jax import lax
from jax.experimental import pallas as pl
from jax.experimental.pallas import tpu as pltpu

F32 = jnp.float32
BF16 = jnp.bfloat16

D_MODEL = 1024
D_SSM = 512
SSM_GROUP = 16
N_SSM_GROUPS = 32
SSM_STATE = 64
D_ATT = 512
HEAD_DIM = 64
N_EXPERTS = 64
TOP_K = 8
N_EXPERT_GROUPS = 8
TOPK_GROUPS = 4
D_EXPERT = 256
ROUTED_SCALE = 2.5
EPS = 1e-6
LAMBDA_RE_MAX = -1e-4

LANES = 128
SUBLANES = 8
VMEM_LIMIT = 56 * 1024 * 1024

N_STATE = N_SSM_GROUPS * SSM_STATE
GROUPS_PER_TILE = LANES // SSM_GROUP
N_LANE_TILES = D_SSM // LANES
STATE_PER_TILE = GROUPS_PER_TILE * SSM_STATE

ROW_TILE = 1024
S5_CHUNK = 128
S5_COLS = 1024
ATT_TILE = LANES
EXP2_UNDERFLOW = 174.0
ATT_LOOKBACK = 2
MOE_BLOCK = 256
CHUNK = 16
BLOCK_CHUNKS = MOE_BLOCK * TOP_K // CHUNK + N_EXPERTS
BLOCK_SLOTS = BLOCK_CHUNKS * CHUNK
PAIRS_PER_BLOCK = BLOCK_CHUNKS // 2
DISPATCH_ROWS = 1024
EXPERT_ROWS = 1024
LOOP_UNROLL = 4

_NT = (((1,), (1,)), ((), ()))


def _rms(x, g):
    ms = jnp.mean(x * x, axis=-1, keepdims=True)
    return x * lax.rsqrt(ms + EPS) * g


def _dot(a, b):
    return jnp.dot(a, b, preferred_element_type=F32)


def _in_proj_kernel(x_ref, g_ref, w_ref, gq_ref, gk_ref, hsum_ref,
                    u_ref, q_ref, k_ref, v_ref):
    hn = _rms(x_ref[...], g_ref[...]).astype(BF16)
    u_ref[...] = _dot(hn, w_ref[:, 0:D_SSM]).astype(BF16)

    def head_norm(lo, g):
        y = _dot(hn, w_ref[:, lo:lo + D_ATT])
        ms = _dot((y * y).astype(BF16), hsum_ref[...])
        return (y * lax.rsqrt(ms + EPS) * g).astype(BF16)

    q_ref[...] = head_norm(D_SSM, gq_ref[...])
    k_ref[...] = head_norm(D_SSM + D_ATT, gk_ref[...])
    v_ref[...] = _dot(hn, w_ref[:, D_SSM + 2 * D_ATT:]).astype(BF16)


def _in_proj(x2, g_mix, w_in, gq, gk, hsum):
    n = x2.shape[0]
    row = lambda i: (i, 0)
    fixed = lambda i: (0, 0)
    out = jax.ShapeDtypeStruct((n, D_SSM), BF16)
    return pl.pallas_call(
        _in_proj_kernel,
        grid=(n // ROW_TILE,),
        in_specs=[
            pl.BlockSpec((ROW_TILE, D_MODEL), row),
            pl.BlockSpec((1, D_MODEL), fixed),
            pl.BlockSpec(w_in.shape, fixed),
            pl.BlockSpec((1, D_ATT), fixed),
            pl.BlockSpec((1, D_ATT), fixed),
            pl.BlockSpec((D_ATT, D_ATT), fixed),
        ],
        out_specs=[pl.BlockSpec((ROW_TILE, D_SSM), row)] * 4,
        out_shape=[out] * 4,
        compiler_params=pltpu.CompilerParams(
            dimension_semantics=("parallel",), vmem_limit_bytes=VMEM_LIMIT),
        name="in_proj",
    )(x2, g_mix, w_in, gq, gk, hsum)


def _gelu_tanh(x):
    c = math.sqrt(2.0 / math.pi)
    return 0.5 * x * (1.0 + jnp.tanh(c * (x + 0.044715 * (x * x * x))))


def _s5_kernel(u_ref, bre_ref, bim_ref, lr_ref, li_ref, cre_ref, cimn_ref,
               d_ref, wg_ref, bg_ref, go_ref, o_ref, sre, sim, hre, him, tb):
    @pl.when(pl.program_id(0) == 0)
    def _():
        hre[...] = jnp.zeros_like(hre)
        him[...] = jnp.zeros_like(him)

    n_batch = u_ref.shape[0]
    tiles = [slice(a * LANES, (a + 1) * LANES) for a in range(N_LANE_TILES)]
    for b in range(n_batch):
        ub = u_ref[b].astype(F32)
        for a, lanes in enumerate(tiles):
            tb[a, pl.ds(b, S5_CHUNK, stride=n_batch), :] = ub[:, lanes]
    u_f32 = jnp.concatenate([tb[a] for a in range(N_LANE_TILES)], axis=1)
    u = u_f32.astype(BF16)
    for a in range(N_LANE_TILES):
        ua = u[:, a * LANES:(a + 1) * LANES]
        cols = slice(a * STATE_PER_TILE, (a + 1) * STATE_PER_TILE)
        sre[:, cols] = _dot(ua, bre_ref[a])
        sim[:, cols] = _dot(ua, bim_ref[a])

    for cb in range(N_STATE // S5_COLS):
        cols = slice(cb * S5_COLS, (cb + 1) * S5_COLS)
        lr = jnp.broadcast_to(lr_ref[:, cols], (SUBLANES, S5_COLS))
        li = jnp.broadcast_to(li_ref[:, cols], (SUBLANES, S5_COLS))

        def step(t, h, cols=cols, lr=lr, li=li):
            hr, hi = h
            rows = pl.ds(pl.multiple_of(t * SUBLANES, SUBLANES), SUBLANES)
            nr = lr * hr - li * hi + sre[rows, cols]
            ni = lr * hi + li * hr + sim[rows, cols]
            sre[rows, cols] = nr
            sim[rows, cols] = ni
            return nr, ni

        hr, hi = lax.fori_loop(0, S5_CHUNK, step, (hre[:, cols], him[:, cols]),
                               unroll=2)
        hre[:, cols] = hr
        him[:, cols] = hi

    ys = []
    for a in range(N_LANE_TILES):
        cols = slice(a * STATE_PER_TILE, (a + 1) * STATE_PER_TILE)
        ys.append(_dot(sre[:, cols].astype(BF16), cre_ref[a])
                  + _dot(sim[:, cols].astype(BF16), cimn_ref[a]))
    y = jnp.concatenate(ys, axis=1)
    y = _gelu_tanh(y + d_ref[...] * u_f32)
    y = y * jax.nn.sigmoid(_dot(y.astype(BF16), wg_ref[...]) + bg_ref[...])
    y = _rms(y, go_ref[...])
    for a, lanes in enumerate(tiles):
        tb[a] = y[:, lanes]
    for b in range(n_batch):
        o_ref[b] = jnp.concatenate(
            [tb[a, pl.ds(b, S5_CHUNK, stride=n_batch), :] for a in range(N_LANE_TILES)],
            axis=1).astype(BF16)


def _s5(u, bre, bim, lr, li, cre, cimn, d_skip, w_glu, b_glu, g_out):
    bsz, seq, _ = u.shape
    rows = S5_CHUNK * bsz
    fixed2 = lambda i: (0, 0)
    fixed3 = lambda i: (0, 0, 0)
    return pl.pallas_call(
        _s5_kernel,
        grid=(seq // S5_CHUNK,),
        in_specs=[
            pl.BlockSpec((bsz, S5_CHUNK, D_SSM), lambda i: (0, i, 0)),
            pl.BlockSpec(bre.shape, fixed3),
            pl.BlockSpec(bim.shape, fixed3),
            pl.BlockSpec((1, N_STATE), fixed2),
            pl.BlockSpec((1, N_STATE), fixed2),
            pl.BlockSpec(cre.shape, fixed3),
            pl.BlockSpec(cimn.shape, fixed3),
            pl.BlockSpec((1, D_SSM), fixed2),
            pl.BlockSpec((D_SSM, D_SSM), fixed2),
            pl.BlockSpec((1, D_SSM), fixed2),
            pl.BlockSpec((1, D_SSM), fixed2),
        ],
        out_specs=pl.BlockSpec((bsz, S5_CHUNK, D_SSM), lambda i: (0, i, 0)),
        out_shape=jax.ShapeDtypeStruct(u.shape, BF16),
        scratch_shapes=[
            pltpu.VMEM((rows, N_STATE), F32),
            pltpu.VMEM((rows, N_STATE), F32),
            pltpu.VMEM((SUBLANES, N_STATE), F32),
            pltpu.VMEM((SUBLANES, N_STATE), F32),
            pltpu.VMEM((N_LANE_TILES, rows, LANES), F32),
        ],
        compiler_params=pltpu.CompilerParams(
            dimension_semantics=("arbitrary",), vmem_limit_bytes=VMEM_LIMIT),
        name="s5_mixer",
    )(u, bre, bim, lr, li, cre, cimn, d_skip, w_glu, b_glu, g_out)


def _s5_weights(lam_re, lam_im, b_re, b_im, c_re, c_im, log_step):
    lre = jnp.minimum(lam_re, LAMBDA_RE_MAX)
    step = jnp.exp(log_step)[:, None]
    mag = jnp.exp(lre * step)
    bar_re = mag * jnp.cos(lam_im * step)
    bar_im = mag * jnp.sin(lam_im * step)
    den = lre * lre + lam_im * lam_im
    coef_re = ((bar_re - 1.0) * lre + bar_im * lam_im) / den
    coef_im = (bar_im * lre - (bar_re - 1.0) * lam_im) / den
    bbar_re = coef_re[..., None] * b_re - coef_im[..., None] * b_im
    bbar_im = coef_re[..., None] * b_im + coef_im[..., None] * b_re
    eye = jnp.eye(GROUPS_PER_TILE, dtype=F32)

    def pack_b(b):
        b = b.reshape(N_LANE_TILES, GROUPS_PER_TILE, SSM_STATE, SSM_GROUP)
        m = jnp.einsum('agph,gk->aghkp', b, eye)
        return m.reshape(N_LANE_TILES, LANES, STATE_PER_TILE).astype(BF16)

    def pack_c(c):
        c = c.reshape(N_LANE_TILES, GROUPS_PER_TILE, SSM_GROUP, SSM_STATE)
        m = jnp.einsum('aghp,gk->agpkh', c, eye)
        return m.reshape(N_LANE_TILES, STATE_PER_TILE, LANES).astype(BF16)

    return (pack_b(bbar_re), pack_b(bbar_im),
            bar_re.reshape(1, N_STATE), bar_im.reshape(1, N_STATE),
            pack_c(c_re), pack_c(-c_im))


def _attn_kernel(q_ref, k_ref, v_ref, tri_ref, o_ref, *scratch):
    acc_refs, car_refs = scratch[:D_ATT // LANES], scratch[D_ATT // LANES:]
    t = ATT_TILE
    i = pl.program_id(1)
    first_head = lax.broadcasted_iota(jnp.int32, (1, LANES), 1) < HEAD_DIM

    def split_heads(ref, rows, lanes):
        x = ref[0, rows, lanes]
        zero = jnp.zeros_like(x)
        return jnp.concatenate([jnp.where(first_head, x, zero),
                                jnp.where(first_head, zero, x)], axis=0)

    def visit(first, count, on_diagonal):
        tiles = [slice(p * LANES, (p + 1) * LANES) for p in range(D_ATT // LANES)]
        rows = [pl.ds(pl.multiple_of((first - b) * t, t), t) for b in range(count)]
        if on_diagonal:
            col = lax.broadcasted_iota(jnp.int32, (t, 2 * t), 1)
            row = lax.broadcasted_iota(jnp.int32, (t, 2 * t), 0)
            strictly_before = jnp.where(col >= t, col - t, col) < row
        masked = lambda b: on_diagonal and b == 0
        zs = [[lax.dot_general(q_ref[0, :, lanes], split_heads(k_ref, r, lanes),
                               _NT, preferred_element_type=F32) for lanes in tiles] for r in rows]
        sps = [[jnp.maximum(z, 0.0) + jnp.log2(1.0 + jnp.exp2(-jnp.abs(z))) for z in zb]
               for zb in zs]
        sps = [[jnp.where(strictly_before, sp, 0.0) if masked(b) else sp for sp in spb]
               for b, spb in enumerate(sps)]
        sums = [[_dot(sp.astype(BF16), tri_ref[...]) for sp in spb] for spb in sps]
        smallest = None
        for p, (lanes, acc_ref, car_ref) in enumerate(zip(tiles, acc_refs, car_refs)):
            car = car_ref[...]
            acc = acc_ref[...]
            for b, r in enumerate(rows):
                w = jnp.exp2(zs[b][p] - sps[b][p] - (sums[b][p][:, :2 * t] + car))
                if masked(b):
                    w = jnp.where(strictly_before, w, 0.0)
                acc = acc + _dot(w.astype(BF16), split_heads(v_ref, r, lanes))
                car = car + sums[b][p][:, 2 * t:]
            acc_ref[...] = acc
            car_ref[...] = car
            smallest = car if smallest is None else jnp.minimum(smallest, car)
        return jnp.min(smallest)

    for ref in scratch:
        ref[...] = jnp.zeros_like(ref)
    passes = [functools.partial(visit, i, c + 1, True) for c in range(ATT_LOOKBACK + 1)]
    carried = lax.switch(jnp.minimum(i, ATT_LOOKBACK), passes)

    def more(state):
        j, carried = state
        return (j >= 0) & (carried < EXP2_UNDERFLOW)

    def earlier(state):
        j, _ = state
        return j - 1, visit(j, 1, False)

    lax.while_loop(more, earlier, (i - 1 - ATT_LOOKBACK, carried))
    o_ref[0] = jnp.concatenate(
        [ref[...] for ref in acc_refs], axis=1).astype(o_ref.dtype)


def _attention(q, k, v, tri):
    bsz, seq, _ = q.shape
    t = ATT_TILE
    n_tiles = D_ATT // LANES
    return pl.pallas_call(
        _attn_kernel,
        grid=(bsz, seq // t),
        in_specs=[
            pl.BlockSpec((1, t, D_ATT), lambda b, i: (b, i, 0)),
            pl.BlockSpec((1, seq, D_ATT), lambda b, i: (b, 0, 0)),
            pl.BlockSpec((1, seq, D_ATT), lambda b, i: (b, 0, 0)),
            pl.BlockSpec(tri.shape, lambda b, i: (0, 0)),
        ],
        out_specs=pl.BlockSpec((1, t, D_ATT), lambda b, i: (b, i, 0)),
        out_shape=jax.ShapeDtypeStruct((bsz, seq, D_ATT), BF16),
        scratch_shapes=([pltpu.VMEM((t, LANES), F32)] * n_tiles
                        + [pltpu.VMEM((t, 2 * t), F32)] * n_tiles),
        compiler_params=pltpu.CompilerParams(
            dimension_semantics=("parallel", "parallel"),
            vmem_limit_bytes=VMEM_LIMIT),
        name="stick_breaking_attention",
    )(q, k, v, tri)


def _pick_first_max(vals, taken, ridx, n):
    cand = jnp.where(taken > 0, -jnp.inf, vals)
    m = jnp.max(cand, axis=0, keepdims=True)
    first = jnp.min(jnp.where((cand == m) & (taken == 0), ridx, n), axis=0, keepdims=True)
    return jnp.where(ridx == first, 1.0, taken)


def _route(scores, biased):
    tm = scores.shape[1]
    per_group = N_EXPERTS // N_EXPERT_GROUPS
    ridx8 = lax.broadcasted_iota(jnp.int32, (per_group, tm), 0)
    zeros8 = jnp.zeros((per_group, tm), F32)
    group_scores = []
    for g in range(N_EXPERT_GROUPS):
        blk = biased[g * per_group:(g + 1) * per_group]
        top1 = _pick_first_max(blk, zeros8, ridx8, per_group)
        top2 = _pick_first_max(blk, top1, ridx8, per_group)
        group_scores.append(jnp.sum(jnp.where(top2 > 0, blk, 0.0), axis=0, keepdims=True))
    grp = jnp.concatenate(group_scores, axis=0)
    ridx_g = lax.broadcasted_iota(jnp.int32, (N_EXPERT_GROUPS, tm), 0)
    grp_taken = jnp.zeros((N_EXPERT_GROUPS, tm), F32)
    for _ in range(TOPK_GROUPS):
        grp_taken = _pick_first_max(grp, grp_taken, ridx_g, N_EXPERT_GROUPS)
    expert_mask = jnp.concatenate(
        [jnp.broadcast_to(grp_taken[g:g + 1], (per_group, tm)) for g in range(N_EXPERT_GROUPS)],
        axis=0)
    sel = jnp.where(expert_mask > 0, biased, -jnp.inf)
    ridx = lax.broadcasted_iota(jnp.int32, (N_EXPERTS, tm), 0)
    taken = jnp.zeros((N_EXPERTS, tm), F32)
    for _ in range(TOP_K):
        taken = _pick_first_max(sel, taken, ridx, N_EXPERTS)
    w = jnp.where(taken > 0, scores, 0.0)
    return w / jnp.sum(w, axis=0, keepdims=True) * ROUTED_SCALE


def _post_mix_kernel(x_ref, ssm_ref, att_ref, ga_ref, wos_ref, woa_ref, gf_ref,
                     wrh_ref, wrl_ref, rb_ref, s1_ref, s3_ref, s2_ref,
                     base_ref, hn_ref, gate_t_ref, cnt_ref):
    att_n = _rms(att_ref[...].astype(F32), ga_ref[...]).astype(BF16)
    h = x_ref[...] + _dot(ssm_ref[...], wos_ref[...]) + _dot(att_n, woa_ref[...])
    hn = _rms(h, gf_ref[...])
    hn_hi = hn.astype(BF16)
    hn_lo = (hn - hn_hi.astype(F32)).astype(BF16)
    hn_ref[...] = hn_hi

    nt = functools.partial(lax.dot_general, dimension_numbers=_NT, preferred_element_type=F32)
    logits = (nt(wrh_ref[...], hn_hi) + nt(wrh_ref[...], hn_lo) + nt(wrl_ref[...], hn_hi))
    scores = jax.nn.sigmoid(logits)
    gate_t = _route(scores, scores + rb_ref[...])
    gate_t_ref[...] = gate_t
    lane = lax.broadcasted_iota(jnp.int32, (1, LANES), 1)
    routed = jnp.where(gate_t > 0.0, 1.0, 0.0)
    counts = jnp.zeros((N_EXPERTS, LANES), F32)
    for blk in range(ROW_TILE // MOE_BLOCK):
        c = jnp.sum(routed[:, blk * MOE_BLOCK:(blk + 1) * MOE_BLOCK], axis=1, keepdims=True)
        counts = jnp.where(lane == blk, c, counts)
    cnt_ref[0] = counts

    a = _dot(hn_hi, s1_ref[...])
    b = _dot(hn_hi, s3_ref[...])
    act = (a * jax.nn.sigmoid(a) * b).astype(BF16)
    base_ref[...] = h + _dot(act, s2_ref[...])


def _post_mix(x2, ssm, att, g_att, wo_s, wo_a, g_ffn, wr_hi, wr_lo, rbias, s1, s3, s2):
    n = x2.shape[0]
    row = lambda i: (i, 0)
    fixed = lambda i: (0, 0)
    full = lambda a: pl.BlockSpec(a.shape, fixed)
    return pl.pallas_call(
        _post_mix_kernel,
        grid=(n // ROW_TILE,),
        in_specs=[
            pl.BlockSpec((ROW_TILE, D_MODEL), row),
            pl.BlockSpec((ROW_TILE, D_SSM), row),
            pl.BlockSpec((ROW_TILE, D_ATT), row),
            full(g_att), full(wo_s), full(wo_a), full(g_ffn),
            full(wr_hi), full(wr_lo), full(rbias), full(s1), full(s3), full(s2),
        ],
        out_specs=[
            pl.BlockSpec((ROW_TILE, D_MODEL), row),
            pl.BlockSpec((ROW_TILE, D_MODEL), row),
            pl.BlockSpec((N_EXPERTS, ROW_TILE), lambda i: (0, i)),
            pl.BlockSpec((1, N_EXPERTS, LANES), lambda i: (i, 0, 0)),
        ],
        out_shape=[
            jax.ShapeDtypeStruct((n, D_MODEL), F32),
            jax.ShapeDtypeStruct((n, D_MODEL), BF16),
            jax.ShapeDtypeStruct((N_EXPERTS, n), F32),
            jax.ShapeDtypeStruct((n // ROW_TILE, N_EXPERTS, LANES), F32),
        ],
        compiler_params=pltpu.CompilerParams(
            dimension_semantics=("parallel",), vmem_limit_bytes=VMEM_LIMIT),
        name="post_mix",
    )(x2, ssm, att, g_att, wo_s, wo_a, g_ffn, wr_hi, wr_lo, rbias, s1, s3, s2)


def _slot_keys(gate_t_ref, earlier_ref, upto_ref, key_ref):
    sel = gate_t_ref[...] > 0.0
    sel_f = jnp.where(sel, 1.0, 0.0)
    rank = _dot(sel_f.astype(BF16), earlier_ref[...])
    count = jnp.sum(sel_f, axis=1, keepdims=True)
    padded = jnp.floor((count + (CHUNK - 1)) * (1.0 / CHUNK)) * CHUNK
    first = _dot(upto_ref[...], jnp.broadcast_to(padded, (N_EXPERTS, LANES)).astype(BF16))
    first = jnp.concatenate([first] * (MOE_BLOCK // LANES), axis=1)
    key_ref[...] = jnp.where(sel, rank + first, -1.0)


def _for_each(n, body):
    trips = n // LOOP_UNROLL

    def unrolled(i, carry):
        for u in range(LOOP_UNROLL):
            body(i * LOOP_UNROLL + u)
        return carry

    def single(i, carry):
        body(i)
        return carry

    lax.fori_loop(0, trips, unrolled, 0)
    lax.fori_loop(trips * LOOP_UNROLL, n, single, 0)


def _fill_slot_rows(p_ref, key_ref, chunk_expert_ref, base, n_chunks, value_ref=None):
    within = lax.broadcasted_iota(jnp.int32, (CHUNK, MOE_BLOCK), 0).astype(F32)

    def fill(c):
        e = chunk_expert_ref[base + c]
        hit = (key_ref[pl.ds(e, 1), :] - (c * CHUNK).astype(F32)) == within
        vals = 1.0 if value_ref is None else value_ref[pl.ds(e, 1), :]
        rows = pl.ds(pl.multiple_of(c * CHUNK, CHUNK), CHUNK)
        p_ref[rows, :] = jnp.where(hit, vals, 0.0).astype(p_ref.dtype)

    _for_each(n_chunks, fill)


def _rows_copy(block_rows_ref, global_ref, c, dst, n, sem, to_global):
    local = block_rows_ref.at[pl.ds(pl.multiple_of(c * CHUNK, CHUNK), n * CHUNK), :]
    remote = global_ref.at[pl.ds(pl.multiple_of(dst * CHUNK, CHUNK), n * CHUNK), :]
    return (pltpu.make_async_copy(local, remote, sem) if to_global
            else pltpu.make_async_copy(remote, local, sem))


def _wait_chunks(block_rows_ref, global_ref, n_chunks, sem, to_global):
    bit = 1 << (BLOCK_CHUNKS.bit_length() - 1)
    while bit:
        @pl.when((n_chunks & bit) != 0)
        def _(bit=bit):
            _rows_copy(block_rows_ref, global_ref, 0, 0, bit, sem, to_global).wait()
        bit >>= 1


def _start_copies(block_rows_ref, global_ref, pairs_ref, singles_ref, counts_ref, blk, sem,
                  to_global):
    for width, list_ref, per_block, which in ((2, pairs_ref, PAIRS_PER_BLOCK, 1),
                                              (1, singles_ref, N_EXPERTS, 2)):
        def start(i, width=width, list_ref=list_ref, per_block=per_block):
            packed = list_ref[blk * per_block + i]
            _rows_copy(block_rows_ref, global_ref, packed & 255, packed >> 8, width, sem,
                       to_global).start()

        _for_each(counts_ref[3 * blk + which], start)


def _dispatch_kernel(chunk_expert_ref, pairs_ref, singles_ref, counts_ref, pad_dst_ref, n_pad_ref,
                     hn_ref, gate_t_ref, earlier_ref, upto_ref, xs_ref,
                     key_ref, p_ref, rows_ref, zero_ref, sems, zero_sem):
    b = pl.program_id(0)
    last = pl.num_programs(0) - 1
    cur = b % 2
    base = b * BLOCK_CHUNKS
    n_chunks = counts_ref[3 * b]

    @pl.when(b == 0)
    def _():
        p_ref[...] = jnp.zeros_like(p_ref)
        zero_ref[...] = jnp.zeros_like(zero_ref)

        def pad_copy(j):
            dst = pl.ds(pl.multiple_of(pad_dst_ref[j] * CHUNK, CHUNK), CHUNK)
            return pltpu.make_async_copy(zero_ref, xs_ref.at[dst, :], zero_sem)

        _for_each(n_pad_ref[0], lambda j: pad_copy(j).start())
        _for_each(n_pad_ref[0], lambda j: pad_copy(j).wait())

    _slot_keys(gate_t_ref, earlier_ref, upto_ref, key_ref)
    _fill_slot_rows(p_ref, key_ref, chunk_expert_ref, base, n_chunks)

    group = DISPATCH_ROWS

    def permute(g, carry):
        rows = pl.ds(pl.multiple_of(g * group, group), group)
        rows_ref[cur, rows, :] = _dot(p_ref[rows, :], hn_ref[...]).astype(BF16)
        return carry

    lax.fori_loop(0, (n_chunks * CHUNK + group - 1) // group, permute, 0)

    _start_copies(rows_ref.at[cur], xs_ref, pairs_ref, singles_ref, counts_ref, b, sems.at[cur],
                  True)

    @pl.when(b > 0)
    def _():
        _wait_chunks(rows_ref.at[1 - cur], xs_ref, counts_ref[3 * jnp.maximum(b - 1, 0)],
                     sems.at[1 - cur], True)

    @pl.when(b == last)
    def _():
        _wait_chunks(rows_ref.at[cur], xs_ref, n_chunks, sems.at[cur], True)


def _dispatch(chunk_expert, pairs, singles, counts, pad_dst, n_pad, hn, gate_t, earlier_t, upto,
              total_rows):
    nb = hn.shape[0] // MOE_BLOCK
    fixed = lambda i, *_: (0, 0)
    return pl.pallas_call(
        _dispatch_kernel,
        grid_spec=pltpu.PrefetchScalarGridSpec(
            num_scalar_prefetch=6,
            grid=(nb,),
            in_specs=[
                pl.BlockSpec((MOE_BLOCK, D_MODEL), lambda i, *_: (i, 0)),
                pl.BlockSpec((N_EXPERTS, MOE_BLOCK), lambda i, *_: (0, i)),
                pl.BlockSpec(earlier_t.shape, fixed),
                pl.BlockSpec(upto.shape, fixed),
            ],
            out_specs=pl.BlockSpec(memory_space=pl.ANY),
            scratch_shapes=[
                pltpu.VMEM((N_EXPERTS, MOE_BLOCK), F32),
                pltpu.VMEM((BLOCK_SLOTS, MOE_BLOCK), BF16),
                pltpu.VMEM((2, BLOCK_SLOTS, D_MODEL), BF16),
                pltpu.VMEM((CHUNK, D_MODEL), BF16),
                pltpu.SemaphoreType.DMA((2,)),
                pltpu.SemaphoreType.DMA(()),
            ],
        ),
        out_shape=jax.ShapeDtypeStruct((total_rows, D_MODEL), BF16),
        compiler_params=pltpu.CompilerParams(
            dimension_semantics=("arbitrary",), vmem_limit_bytes=VMEM_LIMIT),
        name="moe_dispatch",
    )(chunk_expert, pairs, singles, counts, pad_dst, n_pad, hn, gate_t, earlier_t, upto)


def _expert_kernel(tile_expert_ref, n_tiles_ref, x_ref, w1_ref, w3_ref, w2_ref, y_ref):
    @pl.when(pl.program_id(0) < n_tiles_ref[0])
    def _():
        x = x_ref[...]
        a = _dot(x, w1_ref[0].astype(BF16))
        b = _dot(x, w3_ref[0].astype(BF16))
        act = (a * jax.nn.sigmoid(a) * b).astype(BF16)
        y_ref[...] = _dot(act, w2_ref[0].astype(BF16)).astype(BF16)


def _experts(tile_expert, n_tiles, xs, w1, w3, w2):
    rows = lambda m, te, nt: (jnp.minimum(m, nt[0] - 1), 0)
    weight = lambda m, te, nt: (te[m], 0, 0)
    return pl.pallas_call(
        _expert_kernel,
        grid_spec=pltpu.PrefetchScalarGridSpec(
            num_scalar_prefetch=2,
            grid=(xs.shape[0] // EXPERT_ROWS,),
            in_specs=[
                pl.BlockSpec((EXPERT_ROWS, D_MODEL), rows),
                pl.BlockSpec((1, D_MODEL, D_EXPERT), weight),
                pl.BlockSpec((1, D_MODEL, D_EXPERT), weight),
                pl.BlockSpec((1, D_EXPERT, D_MODEL), weight),
            ],
            out_specs=pl.BlockSpec((EXPERT_ROWS, D_MODEL), rows),
        ),
        out_shape=jax.ShapeDtypeStruct(xs.shape, BF16),
        compiler_params=pltpu.CompilerParams(
            dimension_semantics=("arbitrary",), vmem_limit_bytes=VMEM_LIMIT),
        name="moe_experts",
    )(tile_expert, n_tiles, xs, w1, w3, w2)


def _combine_kernel(chunk_expert_ref, pairs_ref, singles_ref, counts_ref,
                    gate_t_ref, earlier_ref, upto_ref, ys_ref, base_ref, o_ref,
                    key_ref, p_ref, rows_ref, sems):
    b = pl.program_id(0)
    last = pl.num_programs(0) - 1
    cur = b % 2
    n_chunks = counts_ref[3 * b]

    def fetch(blk, slot):
        _start_copies(rows_ref.at[slot], ys_ref, pairs_ref, singles_ref, counts_ref, blk,
                      sems.at[slot], False)

    @pl.when(b == 0)
    def _():
        rows_ref[...] = jnp.zeros_like(rows_ref)
        fetch(0, 0)

    @pl.when(b < last)
    def _():
        fetch(jnp.minimum(b + 1, last), 1 - cur)

    group = MOE_BLOCK
    group_chunks = group // CHUNK
    _slot_keys(gate_t_ref, earlier_ref, upto_ref, key_ref)
    _fill_slot_rows(p_ref, key_ref, chunk_expert_ref, b * BLOCK_CHUNKS, n_chunks, gate_t_ref)

    def clear(c):
        rows = pl.ds(pl.multiple_of((n_chunks + c) * CHUNK, CHUNK), CHUNK)
        p_ref[rows, :] = jnp.zeros((CHUNK, MOE_BLOCK), p_ref.dtype)

    _for_each((group_chunks - n_chunks % group_chunks) % group_chunks, clear)
    _wait_chunks(rows_ref.at[cur], ys_ref, n_chunks, sems.at[cur], False)

    o_ref[...] = base_ref[...]

    def gather(g, carry):
        rows = pl.ds(pl.multiple_of(g * group, group), group)
        o_ref[...] += lax.dot_general(p_ref[rows, :].astype(BF16), rows_ref[cur, rows, :],
                                      (((0,), (0,)), ((), ())), preferred_element_type=F32)
        return carry

    lax.fori_loop(0, (n_chunks * CHUNK + group - 1) // group, gather, 0)


def _combine(chunk_expert, pairs, singles, counts, gate_t, earlier_t, upto, ys, base):
    n = base.shape[0]
    fixed = lambda i, *_: (0, 0)
    row = lambda i, *_: (i, 0)
    return pl.pallas_call(
        _combine_kernel,
        grid_spec=pltpu.PrefetchScalarGridSpec(
            num_scalar_prefetch=4,
            grid=(n // MOE_BLOCK,),
            in_specs=[
                pl.BlockSpec((N_EXPERTS, MOE_BLOCK), lambda i, *_: (0, i)),
                pl.BlockSpec(earlier_t.shape, fixed),
                pl.BlockSpec(upto.shape, fixed),
                pl.BlockSpec(memory_space=pl.ANY),
                pl.BlockSpec((MOE_BLOCK, D_MODEL), row),
            ],
            out_specs=pl.BlockSpec((MOE_BLOCK, D_MODEL), row),
            scratch_shapes=[
                pltpu.VMEM((N_EXPERTS, MOE_BLOCK), F32),
                pltpu.VMEM((BLOCK_SLOTS, MOE_BLOCK), F32),
                pltpu.VMEM((2, BLOCK_SLOTS, D_MODEL), BF16),
                pltpu.SemaphoreType.DMA((2,)),
            ],
        ),
        out_shape=jax.ShapeDtypeStruct((n, D_MODEL), F32),
        compiler_params=pltpu.CompilerParams(
            dimension_semantics=("arbitrary",), vmem_limit_bytes=VMEM_LIMIT),
        name="moe_combine",
    )(chunk_expert, pairs, singles, counts, gate_t, earlier_t, upto, ys, base)


def _moe_layout(counts):
    nb = counts.shape[0]
    chunks = (counts + (CHUNK - 1)) // CHUNK
    first = jnp.cumsum(chunks, axis=1) - chunks
    n_chunks = jnp.sum(chunks, axis=1)
    tile_chunks = EXPERT_ROWS // CHUNK
    expert_tiles = (jnp.sum(chunks, axis=0) + tile_chunks - 1) // tile_chunks
    expert_first = (jnp.cumsum(expert_tiles) - expert_tiles) * tile_chunks
    dst = expert_first[None, :] + jnp.cumsum(chunks, axis=0) - chunks
    c = jnp.arange(BLOCK_CHUNKS)
    last = first + chunks
    chunk_expert = jnp.minimum(jnp.sum(c[None, None, :] >= last[:, :, None], axis=1),
                               N_EXPERTS - 1)

    def copy_list(per_run, offset, width, length):
        ends = jnp.cumsum(per_run, axis=1)
        i = jnp.arange(length)
        run = jnp.minimum(jnp.sum(i[None, None, :] >= ends[:, :, None], axis=1), N_EXPERTS - 1)
        owner = run[:, None, :] == jnp.arange(N_EXPERTS)[None, :, None]
        start = offset - width * (ends - per_run)
        packed = (dst + start) * 256 + (first + start)
        return (jnp.sum(jnp.where(owner, packed[:, :, None], 0), axis=1)
                + 257 * width * i[None, :])

    pairs = copy_list(chunks // 2, jnp.zeros_like(chunks), 2, PAIRS_PER_BLOCK)
    singles = copy_list(chunks % 2, chunks - chunks % 2, 1, N_EXPERTS)
    counts3 = jnp.stack([n_chunks, jnp.sum(chunks // 2, axis=1), jnp.sum(chunks % 2, axis=1)],
                        axis=1)
    max_tiles = nb * BLOCK_CHUNKS // tile_chunks + N_EXPERTS
    tile_expert = jnp.minimum(
        jnp.sum(jnp.arange(max_tiles)[:, None] >= jnp.cumsum(expert_tiles)[None, :], axis=1),
        N_EXPERTS - 1)
    expert_chunks = jnp.sum(chunks, axis=0)
    pad = expert_tiles * tile_chunks - expert_chunks
    j = jnp.arange(N_EXPERTS * tile_chunks)
    pad_expert = jnp.minimum(jnp.sum(j[None, :] >= jnp.cumsum(pad)[:, None], axis=0), N_EXPERTS - 1)
    pad_owner = pad_expert[None, :] == jnp.arange(N_EXPERTS)[:, None]
    pad_start = expert_first + expert_chunks - (jnp.cumsum(pad) - pad)
    pad_dst = jnp.sum(jnp.where(pad_owner, pad_start[:, None], 0), axis=0) + j
    i32 = lambda a: a.astype(jnp.int32)
    return (i32(chunk_expert).reshape(-1), i32(pairs).reshape(-1), i32(singles).reshape(-1),
            i32(counts3).reshape(-1), i32(pad_dst), i32(jnp.sum(pad)).reshape(1),
            i32(tile_expert), i32(jnp.sum(expert_tiles)).reshape(1), max_tiles * EXPERT_ROWS)


def _layer(h, g_mix, w_in, lam_re, lam_im, b_re, b_im, c_re, c_im, d_skip, log_step,
           w_glu, b_glu, g_q, g_k, g_out_ssm, g_out_att, w_out, g_ffn, w_router,
           router_bias, e_w1, e_w3, e_w2, s_w1, s_w3, s_w2):
    bsz, seq, d = h.shape
    n = bsz * seq
    x2 = h.reshape(n, d)
    n_heads = D_ATT // HEAD_DIM

    head_id = jnp.arange(D_ATT) // HEAD_DIM
    hsum = jnp.where(head_id[:, None] == head_id[None, :], 1.0 / HEAD_DIM, 0.0).astype(BF16)
    ids = jnp.arange(ATT_TILE)
    later = (ids[:, None] > ids[None, :]).astype(BF16)
    zero, one = jnp.zeros_like(later), jnp.ones_like(later)
    tri = jnp.block([[later, zero, one, zero], [zero, later, zero, one]])

    gq = (jnp.tile(g_q, n_heads) * (math.log2(math.e) / math.sqrt(HEAD_DIM))).reshape(1, D_ATT)
    gk = jnp.tile(g_k, n_heads).reshape(1, D_ATT)
    u, q, k, v = _in_proj(x2, g_mix.reshape(1, d), w_in.astype(BF16), gq, gk, hsum)

    assert bsz == SUBLANES, "the S5 scan keeps one batch entry per sublane"
    s5w = _s5_weights(lam_re, lam_im, b_re, b_im, c_re, c_im, log_step)
    ssm = _s5(u.reshape(bsz, seq, D_SSM), *s5w, d_skip.reshape(1, D_SSM), w_glu.astype(BF16),
              b_glu.reshape(1, D_SSM), g_out_ssm.reshape(1, D_SSM)).reshape(n, D_SSM)

    att = _attention(q.reshape(bsz, seq, D_ATT), k.reshape(bsz, seq, D_ATT),
                     v.reshape(bsz, seq, D_ATT), tri).reshape(n, D_ATT)

    wr_t = w_router.T
    wr_hi = wr_t.astype(BF16)
    wr_lo = (wr_t - wr_hi.astype(F32)).astype(BF16)
    w_out_b = w_out.astype(BF16)
    base, hn, gate_t, counts = _post_mix(
        x2, ssm, att, g_out_att.reshape(1, D_ATT), w_out_b[:D_SSM], w_out_b[D_SSM:],
        g_ffn.reshape(1, d), wr_hi, wr_lo, router_bias.reshape(N_EXPERTS, 1),
        s_w1.astype(BF16), s_w3.astype(BF16), s_w2.astype(BF16))

    w1, w3, w2 = e_w1, e_w3, e_w2
    tok = jnp.arange(MOE_BLOCK)
    earlier_t = (tok[:, None] < tok[None, :]).astype(BF16)
    exp = jnp.arange(N_EXPERTS)
    upto = (exp[None, :] < exp[:, None]).astype(BF16)
    blocks_per_tile = ROW_TILE // MOE_BLOCK
    counts = jnp.round(counts[:, :, :blocks_per_tile]).astype(jnp.int32)
    counts = counts.transpose(0, 2, 1).reshape(n // MOE_BLOCK, N_EXPERTS)
    (chunk_expert, pairs, singles, copy_counts, pad_dst, n_pad, tile_expert, n_tiles,
     total_rows) = _moe_layout(counts)
    xs = _dispatch(chunk_expert, pairs, singles, copy_counts, pad_dst, n_pad, hn, gate_t,
                   earlier_t, upto, total_rows)
    ys = _experts(tile_expert, n_tiles, xs, w1, w3, w2)
    out = _combine(chunk_expert, pairs, singles, copy_counts, gate_t, earlier_t, upto, ys, base)
    return out.reshape(bsz, seq, d)


def kernel(x, g_mix, w_in, lam_re, lam_im, b_re, b_im, c_re, c_im, d_skip, log_step,
           w_glu, b_glu, g_q, g_k, g_out_ssm, g_out_att, w_out, g_ffn, w_router,
           router_bias, e_w1, e_w3, e_w2, s_w1, s_w3, s_w2):
    h = x
    for l in range(g_mix.shape[0]):
        h = _layer(h, g_mix[l], w_in[l], lam_re[l], lam_im[l], b_re[l], b_im[l], c_re[l],
                   c_im[l], d_skip[l], log_step[l], w_glu[l], b_glu[l], g_q[l], g_k[l],
                   g_out_ssm[l], g_out_att[l], w_out[l], g_ffn[l], w_router[l],
                   router_bias[l], e_w1[l], e_w3[l], e_w2[l], s_w1[l], s_w3[l], s_w2[l])
    return h
```

```python
import functools
import math

import jax
import jax.numpy as jnp
from jax import lax
from jax.experimental import pallas as pl
from jax.experimental.pallas import tpu as pltpu

F32 = jnp.float32
BF16 = jnp.bfloat16

D_MODEL = 1024
D_SSM = 512
SSM_GROUP = 16
N_SSM_GROUPS = 32
SSM_STATE = 64
D_ATT = 512
HEAD_DIM = 64
N_EXPERTS = 64
TOP_K = 8
N_EXPERT_GROUPS = 8
TOPK_GROUPS = 4
D_EXPERT = 256
ROUTED_SCALE = 2.5
EPS = 1e-6
LAMBDA_RE_MAX = -1e-4

LANES = 128
SUBLANES = 8
VMEM_LIMIT = 56 * 1024 * 1024

N_STATE = N_SSM_GROUPS * SSM_STATE
GROUPS_PER_TILE = LANES // SSM_GROUP
N_LANE_TILES = D_SSM // LANES
STATE_PER_TILE = GROUPS_PER_TILE * SSM_STATE

ROW_TILE = 1024
S5_CHUNK = 128
S5_COLS = 1024
ATT_TILE = LANES
EXP2_UNDERFLOW = 174.0
ATT_LOOKBACK = 2
MOE_BLOCK = 256
CHUNK = 16
BLOCK_CHUNKS = MOE_BLOCK * TOP_K // CHUNK + N_EXPERTS
BLOCK_SLOTS = BLOCK_CHUNKS * CHUNK
PAIRS_PER_BLOCK = BLOCK_CHUNKS // 2
BLOCK_CHUNK_BITS = (BLOCK_CHUNKS - 1).bit_length()
DISPATCH_ROWS = 1536
COMBINE_ROWS = 1536
EXPERT_ROWS = 1024
LOOP_UNROLL = 4

_NT = (((1,), (1,)), ((), ()))


def _rms(x, g):
    ms = jnp.mean(x * x, axis=-1, keepdims=True)
    return x * lax.rsqrt(ms + EPS) * g


def _dot(a, b):
    return jnp.dot(a, b, preferred_element_type=F32)


def _in_proj_kernel(x_ref, g_ref, w_ref, gq_ref, gk_ref, hsum_ref,
                    u_ref, q_ref, k_ref, v_ref):
    hn = _rms(x_ref[...], g_ref[...]).astype(BF16)
    u_ref[...] = _dot(hn, w_ref[:, 0:D_SSM]).astype(BF16)

    def head_norm(lo, g):
        y = _dot(hn, w_ref[:, lo:lo + D_ATT])
        ms = _dot((y * y).astype(BF16), hsum_ref[...])
        return (y * lax.rsqrt(ms + EPS) * g).astype(BF16)

    q_ref[...] = head_norm(D_SSM, gq_ref[...])
    k_ref[...] = head_norm(D_SSM + D_ATT, gk_ref[...])
    v_ref[...] = _dot(hn, w_ref[:, D_SSM + 2 * D_ATT:]).astype(BF16)


def _in_proj(x2, g_mix, w_in, gq, gk, hsum):
    n = x2.shape[0]
    row = lambda i: (i, 0)
    fixed = lambda i: (0, 0)
    out = jax.ShapeDtypeStruct((n, D_SSM), BF16)
    return pl.pallas_call(
        _in_proj_kernel,
        grid=(n // ROW_TILE,),
        in_specs=[
            pl.BlockSpec((ROW_TILE, D_MODEL), row),
            pl.BlockSpec((1, D_MODEL), fixed),
            pl.BlockSpec(w_in.shape, fixed),
            pl.BlockSpec((1, D_ATT), fixed),
            pl.BlockSpec((1, D_ATT), fixed),
            pl.BlockSpec((D_ATT, D_ATT), fixed),
        ],
        out_specs=[pl.BlockSpec((ROW_TILE, D_SSM), row)] * 4,
        out_shape=[out] * 4,
        compiler_params=pltpu.CompilerParams(
            dimension_semantics=("parallel",), vmem_limit_bytes=VMEM_LIMIT),
        name="in_proj",
    )(x2, g_mix, w_in, gq, gk, hsum)


def _gelu_tanh(x):
    c = math.sqrt(2.0 / math.pi)
    return 0.5 * x * (1.0 + jnp.tanh(c * (x + 0.044715 * (x * x * x))))


def _s5_kernel(u_ref, bre_ref, bim_ref, lr_ref, li_ref, cre_ref, cimn_ref,
               d_ref, wg_ref, bg_ref, go_ref, o_ref, sre, sim, hre, him, tb):
    @pl.when(pl.program_id(0) == 0)
    def _():
        hre[...] = jnp.zeros_like(hre)
        him[...] = jnp.zeros_like(him)

    n_batch = u_ref.shape[0]
    tiles = [slice(a * LANES, (a + 1) * LANES) for a in range(N_LANE_TILES)]
    for b in range(n_batch):
        ub = u_ref[b].astype(F32)
        for a, lanes in enumerate(tiles):
            tb[a, pl.ds(b, S5_CHUNK, stride=n_batch), :] = ub[:, lanes]
    u_f32 = jnp.concatenate([tb[a] for a in range(N_LANE_TILES)], axis=1)
    u = u_f32.astype(BF16)
    for a in range(N_LANE_TILES):
        ua = u[:, a * LANES:(a + 1) * LANES]
        cols = slice(a * STATE_PER_TILE, (a + 1) * STATE_PER_TILE)
        sre[:, cols] = _dot(ua, bre_ref[a])
        sim[:, cols] = _dot(ua, bim_ref[a])

    for cb in range(N_STATE // S5_COLS):
        cols = slice(cb * S5_COLS, (cb + 1) * S5_COLS)
        lr = jnp.broadcast_to(lr_ref[:, cols], (SUBLANES, S5_COLS))
        li = jnp.broadcast_to(li_ref[:, cols], (SUBLANES, S5_COLS))

        def step(t, h, cols=cols, lr=lr, li=li):
            hr, hi = h
            rows = pl.ds(pl.multiple_of(t * SUBLANES, SUBLANES), SUBLANES)
            nr = lr * hr - li * hi + sre[rows, cols]
            ni = lr * hi + li * hr + sim[rows, cols]
            sre[rows, cols] = nr
            sim[rows, cols] = ni
            return nr, ni

        hr, hi = lax.fori_loop(0, S5_CHUNK, step, (hre[:, cols], him[:, cols]),
                               unroll=2)
        hre[:, cols] = hr
        him[:, cols] = hi

    ys = []
    for a in range(N_LANE_TILES):
        cols = slice(a * STATE_PER_TILE, (a + 1) * STATE_PER_TILE)
        ys.append(_dot(sre[:, cols].astype(BF16), cre_ref[a])
                  + _dot(sim[:, cols].astype(BF16), cimn_ref[a]))
    y = jnp.concatenate(ys, axis=1)
    y = _gelu_tanh(y + d_ref[...] * u_f32)
    y = y * jax.nn.sigmoid(_dot(y.astype(BF16), wg_ref[...]) + bg_ref[...])
    y = _rms(y, go_ref[...])
    for a, lanes in enumerate(tiles):
        tb[a] = y[:, lanes]
    for b in range(n_batch):
        o_ref[b] = jnp.concatenate(
            [tb[a, pl.ds(b, S5_CHUNK, stride=n_batch), :] for a in range(N_LANE_TILES)],
            axis=1).astype(BF16)


def _s5(u, bre, bim, lr, li, cre, cimn, d_skip, w_glu, b_glu, g_out):
    bsz, seq, _ = u.shape
    rows = S5_CHUNK * bsz
    fixed2 = lambda i: (0, 0)
    fixed3 = lambda i: (0, 0, 0)
    return pl.pallas_call(
        _s5_kernel,
        grid=(seq // S5_CHUNK,),
        in_specs=[
            pl.BlockSpec((bsz, S5_CHUNK, D_SSM), lambda i: (0, i, 0)),
            pl.BlockSpec(bre.shape, fixed3),
            pl.BlockSpec(bim.shape, fixed3),
            pl.BlockSpec((1, N_STATE), fixed2),
            pl.BlockSpec((1, N_STATE), fixed2),
            pl.BlockSpec(cre.shape, fixed3),
            pl.BlockSpec(cimn.shape, fixed3),
            pl.BlockSpec((1, D_SSM), fixed2),
            pl.BlockSpec((D_SSM, D_SSM), fixed2),
            pl.BlockSpec((1, D_SSM), fixed2),
            pl.BlockSpec((1, D_SSM), fixed2),
        ],
        out_specs=pl.BlockSpec((bsz, S5_CHUNK, D_SSM), lambda i: (0, i, 0)),
        out_shape=jax.ShapeDtypeStruct(u.shape, BF16),
        scratch_shapes=[
            pltpu.VMEM((rows, N_STATE), F32),
            pltpu.VMEM((rows, N_STATE), F32),
            pltpu.VMEM((SUBLANES, N_STATE), F32),
            pltpu.VMEM((SUBLANES, N_STATE), F32),
            pltpu.VMEM((N_LANE_TILES, rows, LANES), F32),
        ],
        compiler_params=pltpu.CompilerParams(
            dimension_semantics=("arbitrary",), vmem_limit_bytes=VMEM_LIMIT),
        name="s5_mixer",
    )(u, bre, bim, lr, li, cre, cimn, d_skip, w_glu, b_glu, g_out)


def _s5_weights(lam_re, lam_im, b_re, b_im, c_re, c_im, log_step):
    lre = jnp.minimum(lam_re, LAMBDA_RE_MAX)
    step = jnp.exp(log_step)[:, None]
    mag = jnp.exp(lre * step)
    bar_re = mag * jnp.cos(lam_im * step)
    bar_im = mag * jnp.sin(lam_im * step)
    den = lre * lre + lam_im * lam_im
    coef_re = ((bar_re - 1.0) * lre + bar_im * lam_im) / den
    coef_im = (bar_im * lre - (bar_re - 1.0) * lam_im) / den
    bbar_re = coef_re[..., None] * b_re - coef_im[..., None] * b_im
    bbar_im = coef_re[..., None] * b_im + coef_im[..., None] * b_re
    eye = jnp.eye(GROUPS_PER_TILE, dtype=F32)

    def pack_b(b):
        b = b.reshape(N_LANE_TILES, GROUPS_PER_TILE, SSM_STATE, SSM_GROUP)
        m = jnp.einsum('agph,gk->aghkp', b, eye)
        return m.reshape(N_LANE_TILES, LANES, STATE_PER_TILE).astype(BF16)

    def pack_c(c):
        c = c.reshape(N_LANE_TILES, GROUPS_PER_TILE, SSM_GROUP, SSM_STATE)
        m = jnp.einsum('aghp,gk->agpkh', c, eye)
        return m.reshape(N_LANE_TILES, STATE_PER_TILE, LANES).astype(BF16)

    return (pack_b(bbar_re), pack_b(bbar_im),
            bar_re.reshape(1, N_STATE), bar_im.reshape(1, N_STATE),
            pack_c(c_re), pack_c(-c_im))


def _attn_kernel(q_ref, k_ref, v_ref, tri_ref, o_ref, *scratch):
    acc_refs, car_refs = scratch[:D_ATT // LANES], scratch[D_ATT // LANES:]
    t = ATT_TILE
    i = pl.program_id(1)
    first_head = lax.broadcasted_iota(jnp.int32, (1, LANES), 1) < HEAD_DIM

    def split_heads(ref, rows, lanes):
        x = ref[0, rows, lanes]
        zero = jnp.zeros_like(x)
        return jnp.concatenate([jnp.where(first_head, x, zero),
                                jnp.where(first_head, zero, x)], axis=0)

    def visit(first, count, on_diagonal):
        tiles = [slice(p * LANES, (p + 1) * LANES) for p in range(D_ATT // LANES)]
        rows = [pl.ds(pl.multiple_of((first - b) * t, t), t) for b in range(count)]
        if on_diagonal:
            col = lax.broadcasted_iota(jnp.int32, (t, 2 * t), 1)
            row = lax.broadcasted_iota(jnp.int32, (t, 2 * t), 0)
            strictly_before = jnp.where(col >= t, col - t, col) < row
        masked = lambda b: on_diagonal and b == 0
        zs = [[lax.dot_general(q_ref[0, :, lanes], split_heads(k_ref, r, lanes),
                               _NT, preferred_element_type=F32) for lanes in tiles] for r in rows]
        sps = [[jnp.maximum(z, 0.0) + jnp.log2(1.0 + jnp.exp2(-jnp.abs(z))) for z in zb]
               for zb in zs]
        sps = [[jnp.where(strictly_before, sp, 0.0) if masked(b) else sp for sp in spb]
               for b, spb in enumerate(sps)]
        sums = [[_dot(sp.astype(BF16), tri_ref[...]) for sp in spb] for spb in sps]
        smallest = None
        for p, (lanes, acc_ref, car_ref) in enumerate(zip(tiles, acc_refs, car_refs)):
            car = car_ref[...]
            acc = acc_ref[...]
            for b, r in enumerate(rows):
                w = jnp.exp2(zs[b][p] - sps[b][p] - (sums[b][p][:, :2 * t] + car))
                if masked(b):
                    w = jnp.where(strictly_before, w, 0.0)
                acc = acc + _dot(w.astype(BF16), split_heads(v_ref, r, lanes))
                car = car + sums[b][p][:, 2 * t:]
            acc_ref[...] = acc
            car_ref[...] = car
            smallest = car if smallest is None else jnp.minimum(smallest, car)
        return jnp.min(smallest)

    for ref in scratch:
        ref[...] = jnp.zeros_like(ref)
    passes = [functools.partial(visit, i, c + 1, True) for c in range(ATT_LOOKBACK + 1)]
    carried = lax.switch(jnp.minimum(i, ATT_LOOKBACK), passes)

    def more(state):
        j, carried = state
        return (j >= 0) & (carried < EXP2_UNDERFLOW)

    def earlier(state):
        j, _ = state
        return j - 1, visit(j, 1, False)

    lax.while_loop(more, earlier, (i - 1 - ATT_LOOKBACK, carried))
    o_ref[0] = jnp.concatenate(
        [ref[...] for ref in acc_refs], axis=1).astype(o_ref.dtype)


def _attention(q, k, v, tri):
    bsz, seq, _ = q.shape
    t = ATT_TILE
    n_tiles = D_ATT // LANES
    return pl.pallas_call(
        _attn_kernel,
        grid=(bsz, seq // t),
        in_specs=[
            pl.BlockSpec((1, t, D_ATT), lambda b, i: (b, i, 0)),
            pl.BlockSpec((1, seq, D_ATT), lambda b, i: (b, 0, 0)),
            pl.BlockSpec((1, seq, D_ATT), lambda b, i: (b, 0, 0)),
            pl.BlockSpec(tri.shape, lambda b, i: (0, 0)),
        ],
        out_specs=pl.BlockSpec((1, t, D_ATT), lambda b, i: (b, i, 0)),
        out_shape=jax.ShapeDtypeStruct((bsz, seq, D_ATT), BF16),
        scratch_shapes=([pltpu.VMEM((t, LANES), F32)] * n_tiles
                        + [pltpu.VMEM((t, 2 * t), F32)] * n_tiles),
        compiler_params=pltpu.CompilerParams(
            dimension_semantics=("parallel", "parallel"),
            vmem_limit_bytes=VMEM_LIMIT),
        name="stick_breaking_attention",
    )(q, k, v, tri)


def _pick_first_max(vals, taken, ridx, n):
    cand = jnp.where(taken > 0, -jnp.inf, vals)
    m = jnp.max(cand, axis=0, keepdims=True)
    first = jnp.min(jnp.where((cand == m) & (taken == 0), ridx, n), axis=0, keepdims=True)
    return jnp.where(ridx == first, 1.0, taken)


def _route(scores, biased):
    tm = scores.shape[1]
    per_group = N_EXPERTS // N_EXPERT_GROUPS
    ridx8 = lax.broadcasted_iota(jnp.int32, (per_group, tm), 0)
    zeros8 = jnp.zeros((per_group, tm), F32)
    group_scores = []
    for g in range(N_EXPERT_GROUPS):
        blk = biased[g * per_group:(g + 1) * per_group]
        top1 = _pick_first_max(blk, zeros8, ridx8, per_group)
        top2 = _pick_first_max(blk, top1, ridx8, per_group)
        group_scores.append(jnp.sum(jnp.where(top2 > 0, blk, 0.0), axis=0, keepdims=True))
    grp = jnp.concatenate(group_scores, axis=0)
    ridx_g = lax.broadcasted_iota(jnp.int32, (N_EXPERT_GROUPS, tm), 0)
    grp_taken = jnp.zeros((N_EXPERT_GROUPS, tm), F32)
    for _ in range(TOPK_GROUPS):
        grp_taken = _pick_first_max(grp, grp_taken, ridx_g, N_EXPERT_GROUPS)
    expert_mask = jnp.concatenate(
        [jnp.broadcast_to(grp_taken[g:g + 1], (per_group, tm)) for g in range(N_EXPERT_GROUPS)],
        axis=0)
    sel = jnp.where(expert_mask > 0, biased, -jnp.inf)
    ridx = lax.broadcasted_iota(jnp.int32, (N_EXPERTS, tm), 0)
    taken = jnp.zeros((N_EXPERTS, tm), F32)
    for _ in range(TOP_K):
        taken = _pick_first_max(sel, taken, ridx, N_EXPERTS)
    w = jnp.where(taken > 0, scores, 0.0)
    return w / jnp.sum(w, axis=0, keepdims=True) * ROUTED_SCALE


def _post_mix_kernel(x_ref, ssm_ref, att_ref, ga_ref, wos_ref, woa_ref, gf_ref,
                     wrh_ref, wrl_ref, rb_ref, s1_ref, s3_ref, s2_ref,
                     base_ref, hn_ref, gate_t_ref, cnt_ref):
    att_n = _rms(att_ref[...].astype(F32), ga_ref[...]).astype(BF16)
    h = x_ref[...] + _dot(ssm_ref[...], wos_ref[...]) + _dot(att_n, woa_ref[...])
    hn = _rms(h, gf_ref[...])
    hn_hi = hn.astype(BF16)
    hn_lo = (hn - hn_hi.astype(F32)).astype(BF16)
    hn_ref[...] = hn_hi

    nt = functools.partial(lax.dot_general, dimension_numbers=_NT, preferred_element_type=F32)
    logits = (nt(wrh_ref[...], hn_hi) + nt(wrh_ref[...], hn_lo) + nt(wrl_ref[...], hn_hi))
    scores = jax.nn.sigmoid(logits)
    gate_t = _route(scores, scores + rb_ref[...])
    gate_t_ref[...] = gate_t
    lane = lax.broadcasted_iota(jnp.int32, (1, LANES), 1)
    routed = jnp.where(gate_t > 0.0, 1.0, 0.0)
    counts = jnp.zeros((N_EXPERTS, LANES), F32)
    for blk in range(ROW_TILE // MOE_BLOCK):
        c = jnp.sum(routed[:, blk * MOE_BLOCK:(blk + 1) * MOE_BLOCK], axis=1, keepdims=True)
        counts = jnp.where(lane == blk, c, counts)
    cnt_ref[0] = counts

    a = _dot(hn_hi, s1_ref[...])
    b = _dot(hn_hi, s3_ref[...])
    act = (a * jax.nn.sigmoid(a) * b).astype(BF16)
    base_ref[...] = h + _dot(act, s2_ref[...])


def _post_mix(x2, ssm, att, g_att, wo_s, wo_a, g_ffn, wr_hi, wr_lo, rbias, s1, s3, s2):
    n = x2.shape[0]
    row = lambda i: (i, 0)
    fixed = lambda i: (0, 0)
    full = lambda a: pl.BlockSpec(a.shape, fixed)
    return pl.pallas_call(
        _post_mix_kernel,
        grid=(n // ROW_TILE,),
        in_specs=[
            pl.BlockSpec((ROW_TILE, D_MODEL), row),
            pl.BlockSpec((ROW_TILE, D_SSM), row),
            pl.BlockSpec((ROW_TILE, D_ATT), row),
            full(g_att), full(wo_s), full(wo_a), full(g_ffn),
            full(wr_hi), full(wr_lo), full(rbias), full(s1), full(s3), full(s2),
        ],
        out_specs=[
            pl.BlockSpec((ROW_TILE, D_MODEL), row),
            pl.BlockSpec((ROW_TILE, D_MODEL), row),
            pl.BlockSpec((N_EXPERTS, ROW_TILE), lambda i: (0, i)),
            pl.BlockSpec((1, N_EXPERTS, LANES), lambda i: (i, 0, 0)),
        ],
        out_shape=[
            jax.ShapeDtypeStruct((n, D_MODEL), F32),
            jax.ShapeDtypeStruct((n, D_MODEL), BF16),
            jax.ShapeDtypeStruct((N_EXPERTS, n), F32),
            jax.ShapeDtypeStruct((n // ROW_TILE, N_EXPERTS, LANES), F32),
        ],
        compiler_params=pltpu.CompilerParams(
            dimension_semantics=("parallel",), vmem_limit_bytes=VMEM_LIMIT),
        name="post_mix",
    )(x2, ssm, att, g_att, wo_s, wo_a, g_ffn, wr_hi, wr_lo, rbias, s1, s3, s2)


def _slot_keys(gate_t_ref, earlier_ref, upto_ref, key_ref):
    sel = gate_t_ref[...] > 0.0
    sel_f = jnp.where(sel, 1.0, 0.0)
    rank = _dot(sel_f.astype(BF16), earlier_ref[...])
    count = jnp.sum(sel_f, axis=1, keepdims=True)
    padded = jnp.floor((count + (CHUNK - 1)) * (1.0 / CHUNK)) * CHUNK
    first = _dot(upto_ref[...], jnp.broadcast_to(padded, (N_EXPERTS, LANES)).astype(BF16))
    first = jnp.concatenate([first] * (MOE_BLOCK // LANES), axis=1)
    key_ref[...] = jnp.where(sel, rank + first, -1.0)


def _for_each(n, body):
    trips = n // LOOP_UNROLL

    def unrolled(i, carry):
        for u in range(LOOP_UNROLL):
            body(i * LOOP_UNROLL + u)
        return carry

    def single(i, carry):
        body(i)
        return carry

    lax.fori_loop(0, trips, unrolled, 0)
    lax.fori_loop(trips * LOOP_UNROLL, n, single, 0)


def _fill_slot_rows(p_ref, key_ref, chunk_expert_ref, base, n_chunks, value_ref=None):
    within = lax.broadcasted_iota(jnp.int32, (CHUNK, MOE_BLOCK), 0).astype(F32)

    def fill(c):
        e = chunk_expert_ref[base + c]
        hit = (key_ref[pl.ds(e, 1), :] - (c * CHUNK).astype(F32)) == within
        vals = 1.0 if value_ref is None else value_ref[pl.ds(e, 1), :]
        rows = pl.ds(pl.multiple_of(c * CHUNK, CHUNK), CHUNK)
        p_ref[rows, :] = jnp.where(hit, vals, 0.0).astype(p_ref.dtype)

    _for_each(n_chunks, fill)


def _rows_copy(block_rows_ref, global_ref, c, dst, n, sem, to_global):
    local = block_rows_ref.at[pl.ds(pl.multiple_of(c * CHUNK, CHUNK), n * CHUNK), :]
    remote = global_ref.at[pl.ds(pl.multiple_of(dst * CHUNK, CHUNK), n * CHUNK), :]
    return (pltpu.make_async_copy(local, remote, sem) if to_global
            else pltpu.make_async_copy(remote, local, sem))


def _wait_chunks(block_rows_ref, global_ref, n_chunks, sem, to_global):
    bit = 1 << (BLOCK_CHUNKS.bit_length() - 1)
    while bit:
        @pl.when((n_chunks & bit) != 0)
        def _(bit=bit):
            _rows_copy(block_rows_ref, global_ref, 0, 0, bit, sem, to_global).wait()
        bit >>= 1


def _start_copies(block_rows_ref, global_ref, pairs_ref, singles_ref, counts_ref, blk, sem,
                  to_global):
    for width, list_ref, per_block, which in ((2, pairs_ref, PAIRS_PER_BLOCK, 1),
                                              (1, singles_ref, N_EXPERTS, 2)):
        def start(i, width=width, list_ref=list_ref, per_block=per_block):
            packed = list_ref[blk * per_block + i]
            _rows_copy(block_rows_ref, global_ref, packed & (1 << BLOCK_CHUNK_BITS) - 1,
                       packed >> BLOCK_CHUNK_BITS, width, sem, to_global).start()

        _for_each(counts_ref[3 * blk + which], start)


def _dispatch_kernel(chunk_expert_ref, pairs_ref, singles_ref, counts_ref, pad_dst_ref, n_pad_ref,
                     hn_ref, gate_t_ref, earlier_ref, upto_ref, xs_ref,
                     key_ref, p_ref, rows_ref, zero_ref, sems, zero_sem):
    b = pl.program_id(0)
    last = pl.num_programs(0) - 1
    cur = b % 2
    base = b * BLOCK_CHUNKS
    n_chunks = counts_ref[3 * b]

    @pl.when(b == 0)
    def _():
        p_ref[...] = jnp.zeros_like(p_ref)
        zero_ref[...] = jnp.zeros_like(zero_ref)

        def pad_copy(j):
            dst = pl.ds(pl.multiple_of(pad_dst_ref[j] * CHUNK, CHUNK), CHUNK)
            return pltpu.make_async_copy(zero_ref, xs_ref.at[dst, :], zero_sem)

        _for_each(n_pad_ref[0], lambda j: pad_copy(j).start())
        _for_each(n_pad_ref[0], lambda j: pad_copy(j).wait())

    _slot_keys(gate_t_ref, earlier_ref, upto_ref, key_ref)
    _fill_slot_rows(p_ref, key_ref, chunk_expert_ref, base, n_chunks)

    group = DISPATCH_ROWS

    def permute(g, carry):
        rows = pl.ds(pl.multiple_of(g * group, group), group)
        rows_ref[cur, rows, :] = _dot(p_ref[rows, :], hn_ref[...]).astype(BF16)
        return carry

    lax.fori_loop(0, (n_chunks * CHUNK + group - 1) // group, permute, 0)

    _start_copies(rows_ref.at[cur], xs_ref, pairs_ref, singles_ref, counts_ref, b, sems.at[cur],
                  True)

    @pl.when(b > 0)
    def _():
        _wait_chunks(rows_ref.at[1 - cur], xs_ref, counts_ref[3 * jnp.maximum(b - 1, 0)],
                     sems.at[1 - cur], True)

    @pl.when(b == last)
    def _():
        _wait_chunks(rows_ref.at[cur], xs_ref, n_chunks, sems.at[cur], True)


def _dispatch(chunk_expert, pairs, singles, counts, pad_dst, n_pad, hn, gate_t, earlier_t, upto,
              total_rows):
    nb = hn.shape[0] // MOE_BLOCK
    fixed = lambda i, *_: (0, 0)
    return pl.pallas_call(
        _dispatch_kernel,
        grid_spec=pltpu.PrefetchScalarGridSpec(
            num_scalar_prefetch=6,
            grid=(nb,),
            in_specs=[
                pl.BlockSpec((MOE_BLOCK, D_MODEL), lambda i, *_: (i, 0)),
                pl.BlockSpec((N_EXPERTS, MOE_BLOCK), lambda i, *_: (0, i)),
                pl.BlockSpec(earlier_t.shape, fixed),
                pl.BlockSpec(upto.shape, fixed),
            ],
            out_specs=pl.BlockSpec(memory_space=pl.ANY),
            scratch_shapes=[
                pltpu.VMEM((N_EXPERTS, MOE_BLOCK), F32),
                pltpu.VMEM((BLOCK_SLOTS, MOE_BLOCK), BF16),
                pltpu.VMEM((2, BLOCK_SLOTS, D_MODEL), BF16),
                pltpu.VMEM((CHUNK, D_MODEL), BF16),
                pltpu.SemaphoreType.DMA((2,)),
                pltpu.SemaphoreType.DMA(()),
            ],
        ),
        out_shape=jax.ShapeDtypeStruct((total_rows, D_MODEL), BF16),
        compiler_params=pltpu.CompilerParams(
            dimension_semantics=("arbitrary",), vmem_limit_bytes=VMEM_LIMIT),
        name="moe_dispatch",
    )(chunk_expert, pairs, singles, counts, pad_dst, n_pad, hn, gate_t, earlier_t, upto)


def _expert_kernel(tile_expert_ref, n_tiles_ref, x_ref, w1_ref, w3_ref, w2_ref, y_ref):
    @pl.when(pl.program_id(0) < n_tiles_ref[0])
    def _():
        x = x_ref[...]
        a = _dot(x, w1_ref[0].astype(BF16))
        b = _dot(x, w3_ref[0].astype(BF16))
        act = (a * jax.nn.sigmoid(a) * b).astype(BF16)
        y_ref[...] = _dot(act, w2_ref[0].astype(BF16)).astype(BF16)


def _experts(tile_expert, n_tiles, xs, w1, w3, w2):
    rows = lambda m, te, nt: (jnp.minimum(m, nt[0] - 1), 0)
    weight = lambda m, te, nt: (te[m], 0, 0)
    return pl.pallas_call(
        _expert_kernel,
        grid_spec=pltpu.PrefetchScalarGridSpec(
            num_scalar_prefetch=2,
            grid=(xs.shape[0] // EXPERT_ROWS,),
            in_specs=[
                pl.BlockSpec((EXPERT_ROWS, D_MODEL), rows),
                pl.BlockSpec((1, D_MODEL, D_EXPERT), weight),
                pl.BlockSpec((1, D_MODEL, D_EXPERT), weight),
                pl.BlockSpec((1, D_EXPERT, D_MODEL), weight),
            ],
            out_specs=pl.BlockSpec((EXPERT_ROWS, D_MODEL), rows),
        ),
        out_shape=jax.ShapeDtypeStruct(xs.shape, BF16),
        compiler_params=pltpu.CompilerParams(
            dimension_semantics=("arbitrary",), vmem_limit_bytes=VMEM_LIMIT),
        name="moe_experts",
    )(tile_expert, n_tiles, xs, w1, w3, w2)


def _combine_kernel(chunk_expert_ref, pairs_ref, singles_ref, counts_ref,
                    gate_t_ref, earlier_ref, upto_ref, ys_ref, base_ref, o_ref,
                    key_ref, p_ref, rows_ref, sems):
    b = pl.program_id(0)
    last = pl.num_programs(0) - 1
    cur = b % 2
    n_chunks = counts_ref[3 * b]

    def fetch(blk, slot):
        _start_copies(rows_ref.at[slot], ys_ref, pairs_ref, singles_ref, counts_ref, blk,
                      sems.at[slot], False)

    @pl.when(b == 0)
    def _():
        rows_ref[...] = jnp.zeros_like(rows_ref)
        fetch(0, 0)

    @pl.when(b < last)
    def _():
        fetch(jnp.minimum(b + 1, last), 1 - cur)

    group = COMBINE_ROWS
    group_chunks = group // CHUNK
    _slot_keys(gate_t_ref, earlier_ref, upto_ref, key_ref)
    _fill_slot_rows(p_ref, key_ref, chunk_expert_ref, b * BLOCK_CHUNKS, n_chunks, gate_t_ref)

    def clear(c):
        rows = pl.ds(pl.multiple_of((n_chunks + c) * CHUNK, CHUNK), CHUNK)
        p_ref[rows, :] = jnp.zeros((CHUNK, MOE_BLOCK), p_ref.dtype)

    _for_each((group_chunks - n_chunks % group_chunks) % group_chunks, clear)
    _wait_chunks(rows_ref.at[cur], ys_ref, n_chunks, sems.at[cur], False)

    o_ref[...] = base_ref[...]

    def gather(g, carry):
        rows = pl.ds(pl.multiple_of(g * group, group), group)
        o_ref[...] += lax.dot_general(p_ref[rows, :].astype(BF16), rows_ref[cur, rows, :],
                                      (((0,), (0,)), ((), ())), preferred_element_type=F32)
        return carry

    lax.fori_loop(0, (n_chunks * CHUNK + group - 1) // group, gather, 0)


def _combine(chunk_expert, pairs, singles, counts, gate_t, earlier_t, upto, ys, base):
    n = base.shape[0]
    fixed = lambda i, *_: (0, 0)
    row = lambda i, *_: (i, 0)
    return pl.pallas_call(
        _combine_kernel,
        grid_spec=pltpu.PrefetchScalarGridSpec(
            num_scalar_prefetch=4,
            grid=(n // MOE_BLOCK,),
            in_specs=[
                pl.BlockSpec((N_EXPERTS, MOE_BLOCK), lambda i, *_: (0, i)),
                pl.BlockSpec(earlier_t.shape, fixed),
                pl.BlockSpec(upto.shape, fixed),
                pl.BlockSpec(memory_space=pl.ANY),
                pl.BlockSpec((MOE_BLOCK, D_MODEL), row),
            ],
            out_specs=pl.BlockSpec((MOE_BLOCK, D_MODEL), row),
            scratch_shapes=[
                pltpu.VMEM((N_EXPERTS, MOE_BLOCK), F32),
                pltpu.VMEM((BLOCK_SLOTS, MOE_BLOCK), F32),
                pltpu.VMEM((2, BLOCK_SLOTS, D_MODEL), BF16),
                pltpu.SemaphoreType.DMA((2,)),
            ],
        ),
        out_shape=jax.ShapeDtypeStruct((n, D_MODEL), F32),
        compiler_params=pltpu.CompilerParams(
            dimension_semantics=("arbitrary",), vmem_limit_bytes=VMEM_LIMIT),
        name="moe_combine",
    )(chunk_expert, pairs, singles, counts, gate_t, earlier_t, upto, ys, base)


def _moe_layout(counts):
    nb = counts.shape[0]
    chunks = (counts + (CHUNK - 1)) // CHUNK
    first = jnp.cumsum(chunks, axis=1) - chunks
    n_chunks = jnp.sum(chunks, axis=1)
    tile_chunks = EXPERT_ROWS // CHUNK
    expert_tiles = (jnp.sum(chunks, axis=0) + tile_chunks - 1) // tile_chunks
    expert_first = (jnp.cumsum(expert_tiles) - expert_tiles) * tile_chunks
    dst = expert_first[None, :] + jnp.cumsum(chunks, axis=0) - chunks
    c = jnp.arange(BLOCK_CHUNKS)
    last = first + chunks
    chunk_expert = jnp.minimum(jnp.sum(c[None, None, :] >= last[:, :, None], axis=1),
                               N_EXPERTS - 1)

    def copy_list(per_run, offset, width, length):
        ends = jnp.cumsum(per_run, axis=1)
        i = jnp.arange(length)
        run = jnp.minimum(jnp.sum(i[None, None, :] >= ends[:, :, None], axis=1), N_EXPERTS - 1)
        owner = run[:, None, :] == jnp.arange(N_EXPERTS)[None, :, None]
        start = offset - width * (ends - per_run)
        both = (1 << BLOCK_CHUNK_BITS) + 1
        packed = (dst << BLOCK_CHUNK_BITS) + first + both * start
        return (jnp.sum(jnp.where(owner, packed[:, :, None], 0), axis=1)
                + both * width * i[None, :])

    pairs = copy_list(chunks // 2, jnp.zeros_like(chunks), 2, PAIRS_PER_BLOCK)
    singles = copy_list(chunks % 2, chunks - chunks % 2, 1, N_EXPERTS)
    counts3 = jnp.stack([n_chunks, jnp.sum(chunks // 2, axis=1), jnp.sum(chunks % 2, axis=1)],
                        axis=1)
    max_tiles = nb * BLOCK_CHUNKS // tile_chunks + N_EXPERTS
    tile_expert = jnp.minimum(
        jnp.sum(jnp.arange(max_tiles)[:, None] >= jnp.cumsum(expert_tiles)[None, :], axis=1),
        N_EXPERTS - 1)
    expert_chunks = jnp.sum(chunks, axis=0)
    pad = expert_tiles * tile_chunks - expert_chunks
    j = jnp.arange(N_EXPERTS * tile_chunks)
    pad_expert = jnp.minimum(jnp.sum(j[None, :] >= jnp.cumsum(pad)[:, None], axis=0), N_EXPERTS - 1)
    pad_owner = pad_expert[None, :] == jnp.arange(N_EXPERTS)[:, None]
    pad_start = expert_first + expert_chunks - (jnp.cumsum(pad) - pad)
    pad_dst = jnp.sum(jnp.where(pad_owner, pad_start[:, None], 0), axis=0) + j
    i32 = lambda a: a.astype(jnp.int32)
    return (i32(chunk_expert).reshape(-1), i32(pairs).reshape(-1), i32(singles).reshape(-1),
            i32(counts3).reshape(-1), i32(pad_dst), i32(jnp.sum(pad)).reshape(1),
            i32(tile_expert), i32(jnp.sum(expert_tiles)).reshape(1), max_tiles * EXPERT_ROWS)


def _layer(h, g_mix, w_in, lam_re, lam_im, b_re, b_im, c_re, c_im, d_skip, log_step,
           w_glu, b_glu, g_q, g_k, g_out_ssm, g_out_att, w_out, g_ffn, w_router,
           router_bias, e_w1, e_w3, e_w2, s_w1, s_w3, s_w2):
    bsz, seq, d = h.shape
    n = bsz * seq
    x2 = h.reshape(n, d)
    n_heads = D_ATT // HEAD_DIM

    head_id = jnp.arange(D_ATT) // HEAD_DIM
    hsum = jnp.where(head_id[:, None] == head_id[None, :], 1.0 / HEAD_DIM, 0.0).astype(BF16)
    ids = jnp.arange(ATT_TILE)
    later = (ids[:, None] > ids[None, :]).astype(BF16)
    zero, one = jnp.zeros_like(later), jnp.ones_like(later)
    tri = jnp.block([[later, zero, one, zero], [zero, later, zero, one]])

    gq = (jnp.tile(g_q, n_heads) * (math.log2(math.e) / math.sqrt(HEAD_DIM))).reshape(1, D_ATT)
    gk = jnp.tile(g_k, n_heads).reshape(1, D_ATT)
    u, q, k, v = _in_proj(x2, g_mix.reshape(1, d), w_in.astype(BF16), gq, gk, hsum)

    assert bsz == SUBLANES, "the S5 scan keeps one batch entry per sublane"
    s5w = _s5_weights(lam_re, lam_im, b_re, b_im, c_re, c_im, log_step)
    ssm = _s5(u.reshape(bsz, seq, D_SSM), *s5w, d_skip.reshape(1, D_SSM), w_glu.astype(BF16),
              b_glu.reshape(1, D_SSM), g_out_ssm.reshape(1, D_SSM)).reshape(n, D_SSM)

    att = _attention(q.reshape(bsz, seq, D_ATT), k.reshape(bsz, seq, D_ATT),
                     v.reshape(bsz, seq, D_ATT), tri).reshape(n, D_ATT)

    wr_t = w_router.T
    wr_hi = wr_t.astype(BF16)
    wr_lo = (wr_t - wr_hi.astype(F32)).astype(BF16)
    w_out_b = w_out.astype(BF16)
    base, hn, gate_t, counts = _post_mix(
        x2, ssm, att, g_out_att.reshape(1, D_ATT), w_out_b[:D_SSM], w_out_b[D_SSM:],
        g_ffn.reshape(1, d), wr_hi, wr_lo, router_bias.reshape(N_EXPERTS, 1),
        s_w1.astype(BF16), s_w3.astype(BF16), s_w2.astype(BF16))

    w1, w3, w2 = e_w1, e_w3, e_w2
    tok = jnp.arange(MOE_BLOCK)
    earlier_t = (tok[:, None] < tok[None, :]).astype(BF16)
    exp = jnp.arange(N_EXPERTS)
    upto = (exp[None, :] < exp[:, None]).astype(BF16)
    blocks_per_tile = ROW_TILE // MOE_BLOCK
    counts = jnp.round(counts[:, :, :blocks_per_tile]).astype(jnp.int32)
    counts = counts.transpose(0, 2, 1).reshape(n // MOE_BLOCK, N_EXPERTS)
    (chunk_expert, pairs, singles, copy_counts, pad_dst, n_pad, tile_expert, n_tiles,
     total_rows) = _moe_layout(counts)
    xs = _dispatch(chunk_expert, pairs, singles, copy_counts, pad_dst, n_pad, hn, gate_t,
                   earlier_t, upto, total_rows)
    ys = _experts(tile_expert, n_tiles, xs, w1, w3, w2)
    out = _combine(chunk_expert, pairs, singles, copy_counts, gate_t, earlier_t, upto, ys, base)
    return out.reshape(bsz, seq, d)


def kernel(x, g_mix, w_in, lam_re, lam_im, b_re, b_im, c_re, c_im, d_skip, log_step,
           w_glu, b_glu, g_q, g_k, g_out_ssm, g_out_att, w_out, g_ffn, w_router,
           router_bias, e_w1, e_w3, e_w2, s_w1, s_w3, s_w2):
    h = x
    for l in range(g_mix.shape[0]):
        h = _layer(h, g_mix[l], w_in[l], lam_re[l], lam_im[l], b_re[l], b_im[l], c_re[l],
                   c_im[l], d_skip[l], log_step[l], w_glu[l], b_glu[l], g_q[l], g_k[l],
                   g_out_ssm[l], g_out_att[l], w_out[l], g_ffn[l], w_router[l],
                   router_bias[l], e_w1[l], e_w3[l], e_w2[l], s_w1[l], s_w3[l], s_w2[l])
    return h
```

```python
import functools
import math

import jax
import jax.numpy as jnp
from jax import lax
from jax.experimental import pallas as pl
from jax.experimental.pallas import tpu as pltpu

F32 = jnp.float32
BF16 = jnp.bfloat16

D_MODEL = 1024
D_SSM = 512
SSM_GROUP = 16
N_SSM_GROUPS = 32
SSM_STATE = 64
D_ATT = 512
HEAD_DIM = 64
N_EXPERTS = 64
TOP_K = 8
N_EXPERT_GROUPS = 8
TOPK_GROUPS = 4
D_EXPERT = 256
ROUTED_SCALE = 2.5
EPS = 1e-6
LAMBDA_RE_MAX = -1e-4

LANES = 128
SUBLANES = 8
VMEM_LIMIT = 56 * 1024 * 1024

N_STATE = N_SSM_GROUPS * SSM_STATE
GROUPS_PER_TILE = LANES // SSM_GROUP
N_LANE_TILES = D_SSM // LANES
STATE_PER_TILE = GROUPS_PER_TILE * SSM_STATE

ROW_TILE = 1024
S5_CHUNK = 128
S5_COLS = 1024
ATT_TILE = LANES
EXP2_UNDERFLOW = 174.0
ATT_LOOKBACK = 2
MOE_BLOCK = 256
CHUNK = 16
BLOCK_CHUNKS = MOE_BLOCK * TOP_K // CHUNK + N_EXPERTS
BLOCK_SLOTS = BLOCK_CHUNKS * CHUNK
PAIRS_PER_BLOCK = BLOCK_CHUNKS // 2
BLOCK_CHUNK_BITS = (BLOCK_CHUNKS - 1).bit_length()
DISPATCH_ROWS = 1536
COMBINE_ROWS = 1536
EXPERT_ROWS = 1024
LOOP_UNROLL = 4

_NT = (((1,), (1,)), ((), ()))


def _rms(x, g):
    ms = jnp.mean(x * x, axis=-1, keepdims=True)
    return x * lax.rsqrt(ms + EPS) * g


def _dot(a, b):
    return jnp.dot(a, b, preferred_element_type=F32)


def _in_proj_kernel(x_ref, g_ref, w_ref, gq_ref, gk_ref, hsum_ref,
                    u_ref, q_ref, k_ref, v_ref):
    hn = _rms(x_ref[...], g_ref[...]).astype(BF16)
    u_ref[...] = _dot(hn, w_ref[:, 0:D_SSM]).astype(BF16)

    def head_norm(lo, g):
        y = _dot(hn, w_ref[:, lo:lo + D_ATT])
        ms = _dot((y * y).astype(BF16), hsum_ref[...])
        return (y * lax.rsqrt(ms + EPS) * g).astype(BF16)

    q_ref[...] = head_norm(D_SSM, gq_ref[...])
    k_ref[...] = head_norm(D_SSM + D_ATT, gk_ref[...])
    v_ref[...] = _dot(hn, w_ref[:, D_SSM + 2 * D_ATT:]).astype(BF16)


def _in_proj(x2, g_mix, w_in, gq, gk, hsum):
    n = x2.shape[0]
    row = lambda i: (i, 0)
    fixed = lambda i: (0, 0)
    out = jax.ShapeDtypeStruct((n, D_SSM), BF16)
    return pl.pallas_call(
        _in_proj_kernel,
        grid=(n // ROW_TILE,),
        in_specs=[
            pl.BlockSpec((ROW_TILE, D_MODEL), row),
            pl.BlockSpec((1, D_MODEL), fixed),
            pl.BlockSpec(w_in.shape, fixed),
            pl.BlockSpec((1, D_ATT), fixed),
            pl.BlockSpec((1, D_ATT), fixed),
            pl.BlockSpec((D_ATT, D_ATT), fixed),
        ],
        out_specs=[pl.BlockSpec((ROW_TILE, D_SSM), row)] * 4,
        out_shape=[out] * 4,
        compiler_params=pltpu.CompilerParams(
            dimension_semantics=("parallel",), vmem_limit_bytes=VMEM_LIMIT),
        name="in_proj",
    )(x2, g_mix, w_in, gq, gk, hsum)


def _gelu_tanh(x):
    c = math.sqrt(2.0 / math.pi)
    return 0.5 * x * (1.0 + jnp.tanh(c * (x + 0.044715 * (x * x * x))))


def _s5_kernel(u_ref, bre_ref, bim_ref, lr_ref, li_ref, cre_ref, cimn_ref,
               d_ref, wg_ref, bg_ref, go_ref, o_ref, sre, sim, hre, him, tb):
    @pl.when(pl.program_id(0) == 0)
    def _():
        hre[...] = jnp.zeros_like(hre)
        him[...] = jnp.zeros_like(him)

    n_batch = u_ref.shape[0]
    tiles = [slice(a * LANES, (a + 1) * LANES) for a in range(N_LANE_TILES)]
    for b in range(n_batch):
        ub = u_ref[b].astype(F32)
        for a, lanes in enumerate(tiles):
            tb[a, pl.ds(b, S5_CHUNK, stride=n_batch), :] = ub[:, lanes]
    u_f32 = jnp.concatenate([tb[a] for a in range(N_LANE_TILES)], axis=1)
    u = u_f32.astype(BF16)
    for a in range(N_LANE_TILES):
        ua = u[:, a * LANES:(a + 1) * LANES]
        cols = slice(a * STATE_PER_TILE, (a + 1) * STATE_PER_TILE)
        sre[:, cols] = _dot(ua, bre_ref[a])
        sim[:, cols] = _dot(ua, bim_ref[a])

    for cb in range(N_STATE // S5_COLS):
        cols = slice(cb * S5_COLS, (cb + 1) * S5_COLS)
        lr = jnp.broadcast_to(lr_ref[:, cols], (SUBLANES, S5_COLS))
        li = jnp.broadcast_to(li_ref[:, cols], (SUBLANES, S5_COLS))

        def step(t, h, cols=cols, lr=lr, li=li):
            hr, hi = h
            rows = pl.ds(pl.multiple_of(t * SUBLANES, SUBLANES), SUBLANES)
            nr = lr * hr - li * hi + sre[rows, cols]
            ni = lr * hi + li * hr + sim[rows, cols]
            sre[rows, cols] = nr
            sim[rows, cols] = ni
            return nr, ni

        hr, hi = lax.fori_loop(0, S5_CHUNK, step, (hre[:, cols], him[:, cols]),
                               unroll=2)
        hre[:, cols] = hr
        him[:, cols] = hi

    ys = []
    for a in range(N_LANE_TILES):
        cols = slice(a * STATE_PER_TILE, (a + 1) * STATE_PER_TILE)
        ys.append(_dot(sre[:, cols].astype(BF16), cre_ref[a])
                  + _dot(sim[:, cols].astype(BF16), cimn_ref[a]))
    y = jnp.concatenate(ys, axis=1)
    y = _gelu_tanh(y + d_ref[...] * u_f32)
    y = y * jax.nn.sigmoid(_dot(y.astype(BF16), wg_ref[...]) + bg_ref[...])
    y = _rms(y, go_ref[...])
    for a, lanes in enumerate(tiles):
        tb[a] = y[:, lanes]
    for b in range(n_batch):
        o_ref[b] = jnp.concatenate(
            [tb[a, pl.ds(b, S5_CHUNK, stride=n_batch), :] for a in range(N_LANE_TILES)],
            axis=1).astype(BF16)


def _s5(u, bre, bim, lr, li, cre, cimn, d_skip, w_glu, b_glu, g_out):
    bsz, seq, _ = u.shape
    rows = S5_CHUNK * bsz
    fixed2 = lambda i: (0, 0)
    fixed3 = lambda i: (0, 0, 0)
    return pl.pallas_call(
        _s5_kernel,
        grid=(seq // S5_CHUNK,),
        in_specs=[
            pl.BlockSpec((bsz, S5_CHUNK, D_SSM), lambda i: (0, i, 0)),
            pl.BlockSpec(bre.shape, fixed3),
            pl.BlockSpec(bim.shape, fixed3),
            pl.BlockSpec((1, N_STATE), fixed2),
            pl.BlockSpec((1, N_STATE), fixed2),
            pl.BlockSpec(cre.shape, fixed3),
            pl.BlockSpec(cimn.shape, fixed3),
            pl.BlockSpec((1, D_SSM), fixed2),
            pl.BlockSpec((D_SSM, D_SSM), fixed2),
            pl.BlockSpec((1, D_SSM), fixed2),
            pl.BlockSpec((1, D_SSM), fixed2),
        ],
        out_specs=pl.BlockSpec((bsz, S5_CHUNK, D_SSM), lambda i: (0, i, 0)),
        out_shape=jax.ShapeDtypeStruct(u.shape, BF16),
        scratch_shapes=[
            pltpu.VMEM((rows, N_STATE), F32),
            pltpu.VMEM((rows, N_STATE), F32),
            pltpu.VMEM((SUBLANES, N_STATE), F32),
            pltpu.VMEM((SUBLANES, N_STATE), F32),
            pltpu.VMEM((N_LANE_TILES, rows, LANES), F32),
        ],
        compiler_params=pltpu.CompilerParams(
            dimension_semantics=("arbitrary",), vmem_limit_bytes=VMEM_LIMIT),
        name="s5_mixer",
    )(u, bre, bim, lr, li, cre, cimn, d_skip, w_glu, b_glu, g_out)


def _s5_weights(lam_re, lam_im, b_re, b_im, c_re, c_im, log_step):
    lre = jnp.minimum(lam_re, LAMBDA_RE_MAX)
    step = jnp.exp(log_step)[:, None]
    mag = jnp.exp(lre * step)
    bar_re = mag * jnp.cos(lam_im * step)
    bar_im = mag * jnp.sin(lam_im * step)
    den = lre * lre + lam_im * lam_im
    coef_re = ((bar_re - 1.0) * lre + bar_im * lam_im) / den
    coef_im = (bar_im * lre - (bar_re - 1.0) * lam_im) / den
    bbar_re = coef_re[..., None] * b_re - coef_im[..., None] * b_im
    bbar_im = coef_re[..., None] * b_im + coef_im[..., None] * b_re
    eye = jnp.eye(GROUPS_PER_TILE, dtype=F32)

    def pack_b(b):
        b = b.reshape(N_LANE_TILES, GROUPS_PER_TILE, SSM_STATE, SSM_GROUP)
        m = jnp.einsum('agph,gk->aghkp', b, eye)
        return m.reshape(N_LANE_TILES, LANES, STATE_PER_TILE).astype(BF16)

    def pack_c(c):
        c = c.reshape(N_LANE_TILES, GROUPS_PER_TILE, SSM_GROUP, SSM_STATE)
        m = jnp.einsum('aghp,gk->agpkh', c, eye)
        return m.reshape(N_LANE_TILES, STATE_PER_TILE, LANES).astype(BF16)

    return (pack_b(bbar_re), pack_b(bbar_im),
            bar_re.reshape(1, N_STATE), bar_im.reshape(1, N_STATE),
            pack_c(c_re), pack_c(-c_im))


def _attn_kernel(q_ref, k_ref, v_ref, tri_ref, o_ref, *scratch):
    acc_refs, car_refs = scratch[:D_ATT // LANES], scratch[D_ATT // LANES:]
    t = ATT_TILE
    i = pl.program_id(1)
    first_head = lax.broadcasted_iota(jnp.int32, (1, LANES), 1) < HEAD_DIM

    def split_heads(ref, rows, lanes):
        x = ref[0, rows, lanes]
        zero = jnp.zeros_like(x)
        return jnp.concatenate([jnp.where(first_head, x, zero),
                                jnp.where(first_head, zero, x)], axis=0)

    def visit(first, count, on_diagonal):
        tiles = [slice(p * LANES, (p + 1) * LANES) for p in range(D_ATT // LANES)]
        rows = [pl.ds(pl.multiple_of((first - b) * t, t), t) for b in range(count)]
        if on_diagonal:
            col = lax.broadcasted_iota(jnp.int32, (t, 2 * t), 1)
            row = lax.broadcasted_iota(jnp.int32, (t, 2 * t), 0)
            strictly_before = jnp.where(col >= t, col - t, col) < row
        masked = lambda b: on_diagonal and b == 0
        zs = [[lax.dot_general(q_ref[0, :, lanes], split_heads(k_ref, r, lanes),
                               _NT, preferred_element_type=F32) for lanes in tiles] for r in rows]
        sps = [[jnp.maximum(z, 0.0) + jnp.log2(1.0 + jnp.exp2(-jnp.abs(z))) for z in zb]
               for zb in zs]
        sps = [[jnp.where(strictly_before, sp, 0.0) if masked(b) else sp for sp in spb]
               for b, spb in enumerate(sps)]
        stacked = _dot(jnp.concatenate([sp.astype(BF16) for spb in sps for sp in spb], axis=0),
                       tri_ref[...])
        sums = [[stacked[(b * len(tiles) + p) * t:(b * len(tiles) + p + 1) * t]
                 for p in range(len(tiles))] for b in range(count)]
        smallest = None
        for p, (lanes, acc_ref, car_ref) in enumerate(zip(tiles, acc_refs, car_refs)):
            car = car_ref[...]
            acc = acc_ref[...]
            for b, r in enumerate(rows):
                w = jnp.exp2(zs[b][p] - sps[b][p] - (sums[b][p][:, :2 * t] + car))
                if masked(b):
                    w = jnp.where(strictly_before, w, 0.0)
                acc = acc + _dot(w.astype(BF16), split_heads(v_ref, r, lanes))
                car = car + sums[b][p][:, 2 * t:]
            acc_ref[...] = acc
            car_ref[...] = car
            smallest = car if smallest is None else jnp.minimum(smallest, car)
        return jnp.min(smallest)

    for ref in scratch:
        ref[...] = jnp.zeros_like(ref)
    passes = [functools.partial(visit, i, c + 1, True) for c in range(ATT_LOOKBACK + 1)]
    carried = lax.switch(jnp.minimum(i, ATT_LOOKBACK), passes)

    def more(state):
        j, carried = state
        return (j >= 0) & (carried < EXP2_UNDERFLOW)

    def earlier(state):
        j, _ = state
        return j - 1, visit(j, 1, False)

    lax.while_loop(more, earlier, (i - 1 - ATT_LOOKBACK, carried))
    o_ref[0] = jnp.concatenate(
        [ref[...] for ref in acc_refs], axis=1).astype(o_ref.dtype)


def _attention(q, k, v, tri):
    bsz, seq, _ = q.shape
    t = ATT_TILE
    n_tiles = D_ATT // LANES
    return pl.pallas_call(
        _attn_kernel,
        grid=(bsz, seq // t),
        in_specs=[
            pl.BlockSpec((1, t, D_ATT), lambda b, i: (b, i, 0)),
            pl.BlockSpec((1, seq, D_ATT), lambda b, i: (b, 0, 0)),
            pl.BlockSpec((1, seq, D_ATT), lambda b, i: (b, 0, 0)),
            pl.BlockSpec(tri.shape, lambda b, i: (0, 0)),
        ],
        out_specs=pl.BlockSpec((1, t, D_ATT), lambda b, i: (b, i, 0)),
        out_shape=jax.ShapeDtypeStruct((bsz, seq, D_ATT), BF16),
        scratch_shapes=([pltpu.VMEM((t, LANES), F32)] * n_tiles
                        + [pltpu.VMEM((t, 2 * t), F32)] * n_tiles),
        compiler_params=pltpu.CompilerParams(
            dimension_semantics=("parallel", "parallel"),
            vmem_limit_bytes=VMEM_LIMIT),
        name="stick_breaking_attention",
    )(q, k, v, tri)


def _pick_first_max(vals, taken, ridx, n):
    cand = jnp.where(taken > 0, -jnp.inf, vals)
    m = jnp.max(cand, axis=0, keepdims=True)
    first = jnp.min(jnp.where((cand == m) & (taken == 0), ridx, n), axis=0, keepdims=True)
    return jnp.where(ridx == first, 1.0, taken)


def _route(scores, biased):
    tm = scores.shape[1]
    per_group = N_EXPERTS // N_EXPERT_GROUPS
    ridx8 = lax.broadcasted_iota(jnp.int32, (per_group, tm), 0)
    zeros8 = jnp.zeros((per_group, tm), F32)
    group_scores = []
    for g in range(N_EXPERT_GROUPS):
        blk = biased[g * per_group:(g + 1) * per_group]
        top1 = _pick_first_max(blk, zeros8, ridx8, per_group)
        top2 = _pick_first_max(blk, top1, ridx8, per_group)
        group_scores.append(jnp.sum(jnp.where(top2 > 0, blk, 0.0), axis=0, keepdims=True))
    grp = jnp.concatenate(group_scores, axis=0)
    ridx_g = lax.broadcasted_iota(jnp.int32, (N_EXPERT_GROUPS, tm), 0)
    grp_taken = jnp.zeros((N_EXPERT_GROUPS, tm), F32)
    for _ in range(TOPK_GROUPS):
        grp_taken = _pick_first_max(grp, grp_taken, ridx_g, N_EXPERT_GROUPS)
    expert_mask = jnp.concatenate(
        [jnp.broadcast_to(grp_taken[g:g + 1], (per_group, tm)) for g in range(N_EXPERT_GROUPS)],
        axis=0)
    sel = jnp.where(expert_mask > 0, biased, -jnp.inf)
    ridx = lax.broadcasted_iota(jnp.int32, (N_EXPERTS, tm), 0)
    taken = jnp.zeros((N_EXPERTS, tm), F32)
    for _ in range(TOP_K):
        taken = _pick_first_max(sel, taken, ridx, N_EXPERTS)
    w = jnp.where(taken > 0, scores, 0.0)
    return w / jnp.sum(w, axis=0, keepdims=True) * ROUTED_SCALE


def _post_mix_kernel(x_ref, ssm_ref, att_ref, ga_ref, wos_ref, woa_ref, gf_ref,
                     wrh_ref, wrl_ref, rb_ref, s1_ref, s3_ref, s2_ref,
                     base_ref, hn_ref, gate_t_ref, cnt_ref):
    att_n = _rms(att_ref[...].astype(F32), ga_ref[...]).astype(BF16)
    h = x_ref[...] + _dot(ssm_ref[...], wos_ref[...]) + _dot(att_n, woa_ref[...])
    hn = _rms(h, gf_ref[...])
    hn_hi = hn.astype(BF16)
    hn_lo = (hn - hn_hi.astype(F32)).astype(BF16)
    hn_ref[...] = hn_hi

    nt = functools.partial(lax.dot_general, dimension_numbers=_NT, preferred_element_type=F32)
    logits = (nt(wrh_ref[...], hn_hi) + nt(wrh_ref[...], hn_lo) + nt(wrl_ref[...], hn_hi))
    scores = jax.nn.sigmoid(logits)
    gate_t = _route(scores, scores + rb_ref[...])
    gate_t_ref[...] = gate_t
    lane = lax.broadcasted_iota(jnp.int32, (1, LANES), 1)
    routed = jnp.where(gate_t > 0.0, 1.0, 0.0)
    counts = jnp.zeros((N_EXPERTS, LANES), F32)
    for blk in range(ROW_TILE // MOE_BLOCK):
        c = jnp.sum(routed[:, blk * MOE_BLOCK:(blk + 1) * MOE_BLOCK], axis=1, keepdims=True)
        counts = jnp.where(lane == blk, c, counts)
    cnt_ref[0] = counts

    a = _dot(hn_hi, s1_ref[...])
    b = _dot(hn_hi, s3_ref[...])
    act = (a * jax.nn.sigmoid(a) * b).astype(BF16)
    base_ref[...] = h + _dot(act, s2_ref[...])


def _post_mix(x2, ssm, att, g_att, wo_s, wo_a, g_ffn, wr_hi, wr_lo, rbias, s1, s3, s2):
    n = x2.shape[0]
    row = lambda i: (i, 0)
    fixed = lambda i: (0, 0)
    full = lambda a: pl.BlockSpec(a.shape, fixed)
    return pl.pallas_call(
        _post_mix_kernel,
        grid=(n // ROW_TILE,),
        in_specs=[
            pl.BlockSpec((ROW_TILE, D_MODEL), row),
            pl.BlockSpec((ROW_TILE, D_SSM), row),
            pl.BlockSpec((ROW_TILE, D_ATT), row),
            full(g_att), full(wo_s), full(wo_a), full(g_ffn),
            full(wr_hi), full(wr_lo), full(rbias), full(s1), full(s3), full(s2),
        ],
        out_specs=[
            pl.BlockSpec((ROW_TILE, D_MODEL), row),
            pl.BlockSpec((ROW_TILE, D_MODEL), row),
            pl.BlockSpec((N_EXPERTS, ROW_TILE), lambda i: (0, i)),
            pl.BlockSpec((1, N_EXPERTS, LANES), lambda i: (i, 0, 0)),
        ],
        out_shape=[
            jax.ShapeDtypeStruct((n, D_MODEL), F32),
            jax.ShapeDtypeStruct((n, D_MODEL), BF16),
            jax.ShapeDtypeStruct((N_EXPERTS, n), F32),
            jax.ShapeDtypeStruct((n // ROW_TILE, N_EXPERTS, LANES), F32),
        ],
        compiler_params=pltpu.CompilerParams(
            dimension_semantics=("parallel",), vmem_limit_bytes=VMEM_LIMIT),
        name="post_mix",
    )(x2, ssm, att, g_att, wo_s, wo_a, g_ffn, wr_hi, wr_lo, rbias, s1, s3, s2)


def _slot_keys(gate_t_ref, earlier_ref, upto_ref, key_ref):
    sel = gate_t_ref[...] > 0.0
    sel_f = jnp.where(sel, 1.0, 0.0)
    rank = _dot(sel_f.astype(BF16), earlier_ref[...])
    count = jnp.sum(sel_f, axis=1, keepdims=True)
    padded = jnp.floor((count + (CHUNK - 1)) * (1.0 / CHUNK)) * CHUNK
    first = _dot(upto_ref[...], jnp.broadcast_to(padded, (N_EXPERTS, LANES)).astype(BF16))
    first = jnp.concatenate([first] * (MOE_BLOCK // LANES), axis=1)
    key_ref[...] = jnp.where(sel, rank + first, -1.0)


def _for_each(n, body):
    trips = n // LOOP_UNROLL

    def unrolled(i, carry):
        for u in range(LOOP_UNROLL):
            body(i * LOOP_UNROLL + u)
        return carry

    def single(i, carry):
        body(i)
        return carry

    lax.fori_loop(0, trips, unrolled, 0)
    lax.fori_loop(trips * LOOP_UNROLL, n, single, 0)


def _fill_slot_rows(p_ref, key_ref, chunk_expert_ref, base, n_chunks, value_ref=None):
    within = lax.broadcasted_iota(jnp.int32, (CHUNK, MOE_BLOCK), 0).astype(F32)

    def fill(c):
        e = chunk_expert_ref[base + c]
        hit = (key_ref[pl.ds(e, 1), :] - (c * CHUNK).astype(F32)) == within
        vals = 1.0 if value_ref is None else value_ref[pl.ds(e, 1), :]
        rows = pl.ds(pl.multiple_of(c * CHUNK, CHUNK), CHUNK)
        p_ref[rows, :] = jnp.where(hit, vals, 0.0).astype(p_ref.dtype)

    _for_each(n_chunks, fill)


def _rows_copy(block_rows_ref, global_ref, c, dst, n, sem, to_global):
    local = block_rows_ref.reshape(BLOCK_CHUNKS, CHUNK, D_MODEL).at[pl.ds(c, n)]
    remote = global_ref.reshape(global_ref.shape[0] // CHUNK, CHUNK, D_MODEL).at[pl.ds(dst, n)]
    return (pltpu.make_async_copy(local, remote, sem) if to_global
            else pltpu.make_async_copy(remote, local, sem))


def _wait_chunks(block_rows_ref, global_ref, n_chunks, sem, to_global):
    bit = 1 << (BLOCK_CHUNKS.bit_length() - 1)
    while bit:
        @pl.when((n_chunks & bit) != 0)
        def _(bit=bit):
            _rows_copy(block_rows_ref, global_ref, 0, 0, bit, sem, to_global).wait()
        bit >>= 1


def _start_copies(block_rows_ref, global_ref, pairs_ref, singles_ref, counts_ref, blk, sem,
                  to_global):
    for width, list_ref, per_block, which in ((2, pairs_ref, PAIRS_PER_BLOCK, 1),
                                              (1, singles_ref, N_EXPERTS, 2)):
        def start(i, width=width, list_ref=list_ref, per_block=per_block):
            packed = list_ref[blk * per_block + i]
            _rows_copy(block_rows_ref, global_ref, packed & (1 << BLOCK_CHUNK_BITS) - 1,
                       packed >> BLOCK_CHUNK_BITS, width, sem, to_global).start()

        _for_each(counts_ref[3 * blk + which], start)


def _dispatch_kernel(chunk_expert_ref, pairs_ref, singles_ref, counts_ref, pad_dst_ref, n_pad_ref,
                     hn_ref, gate_t_ref, earlier_ref, upto_ref, xs_ref,
                     key_ref, p_ref, rows_ref, zero_ref, sems, zero_sem):
    b = pl.program_id(0)
    last = pl.num_programs(0) - 1
    cur = b % 2
    base = b * BLOCK_CHUNKS
    n_chunks = counts_ref[3 * b]

    @pl.when(b == 0)
    def _():
        p_ref[...] = jnp.zeros_like(p_ref)
        zero_ref[...] = jnp.zeros_like(zero_ref)

        def pad_copy(j):
            dst = pl.ds(pl.multiple_of(pad_dst_ref[j] * CHUNK, CHUNK), CHUNK)
            return pltpu.make_async_copy(zero_ref, xs_ref.at[dst, :], zero_sem)

        _for_each(n_pad_ref[0], lambda j: pad_copy(j).start())
        _for_each(n_pad_ref[0], lambda j: pad_copy(j).wait())

    _slot_keys(gate_t_ref, earlier_ref, upto_ref, key_ref)
    _fill_slot_rows(p_ref, key_ref, chunk_expert_ref, base, n_chunks)

    group = DISPATCH_ROWS

    def permute(g, carry):
        rows = pl.ds(pl.multiple_of(g * group, group), group)
        rows_ref[cur, rows, :] = _dot(p_ref[rows, :], hn_ref[...]).astype(BF16)
        return carry

    lax.fori_loop(0, (n_chunks * CHUNK + group - 1) // group, permute, 0)

    _start_copies(rows_ref.at[cur], xs_ref, pairs_ref, singles_ref, counts_ref, b, sems.at[cur],
                  True)

    @pl.when(b > 0)
    def _():
        _wait_chunks(rows_ref.at[1 - cur], xs_ref, counts_ref[3 * jnp.maximum(b - 1, 0)],
                     sems.at[1 - cur], True)

    @pl.when(b == last)
    def _():
        _wait_chunks(rows_ref.at[cur], xs_ref, n_chunks, sems.at[cur], True)


def _dispatch(chunk_expert, pairs, singles, counts, pad_dst, n_pad, hn, gate_t, earlier_t, upto,
              total_rows):
    nb = hn.shape[0] // MOE_BLOCK
    fixed = lambda i, *_: (0, 0)
    return pl.pallas_call(
        _dispatch_kernel,
        grid_spec=pltpu.PrefetchScalarGridSpec(
            num_scalar_prefetch=6,
            grid=(nb,),
            in_specs=[
                pl.BlockSpec((MOE_BLOCK, D_MODEL), lambda i, *_: (i, 0)),
                pl.BlockSpec((N_EXPERTS, MOE_BLOCK), lambda i, *_: (0, i)),
                pl.BlockSpec(earlier_t.shape, fixed),
                pl.BlockSpec(upto.shape, fixed),
            ],
            out_specs=pl.BlockSpec(memory_space=pl.ANY),
            scratch_shapes=[
                pltpu.VMEM((N_EXPERTS, MOE_BLOCK), F32),
                pltpu.VMEM((BLOCK_SLOTS, MOE_BLOCK), BF16),
                pltpu.VMEM((2, BLOCK_SLOTS, D_MODEL), BF16),
                pltpu.VMEM((CHUNK, D_MODEL), BF16),
                pltpu.SemaphoreType.DMA((2,)),
                pltpu.SemaphoreType.DMA(()),
            ],
        ),
        out_shape=jax.ShapeDtypeStruct((total_rows, D_MODEL), BF16),
        compiler_params=pltpu.CompilerParams(
            dimension_semantics=("arbitrary",), vmem_limit_bytes=VMEM_LIMIT),
        name="moe_dispatch",
    )(chunk_expert, pairs, singles, counts, pad_dst, n_pad, hn, gate_t, earlier_t, upto)


def _expert_kernel(tile_expert_ref, n_tiles_ref, x_ref, w1_ref, w3_ref, w2_ref, y_ref):
    @pl.when(pl.program_id(0) < n_tiles_ref[0])
    def _():
        x = x_ref[...]
        a = _dot(x, w1_ref[0].astype(BF16))
        b = _dot(x, w3_ref[0].astype(BF16))
        act = (a * jax.nn.sigmoid(a) * b).astype(BF16)
        y_ref[...] = _dot(act, w2_ref[0].astype(BF16)).astype(BF16)


def _experts(tile_expert, n_tiles, xs, w1, w3, w2):
    rows = lambda m, te, nt: (jnp.minimum(m, nt[0] - 1), 0)
    weight = lambda m, te, nt: (te[m], 0, 0)
    return pl.pallas_call(
        _expert_kernel,
        grid_spec=pltpu.PrefetchScalarGridSpec(
            num_scalar_prefetch=2,
            grid=(xs.shape[0] // EXPERT_ROWS,),
            in_specs=[
                pl.BlockSpec((EXPERT_ROWS, D_MODEL), rows),
                pl.BlockSpec((1, D_MODEL, D_EXPERT), weight),
                pl.BlockSpec((1, D_MODEL, D_EXPERT), weight),
                pl.BlockSpec((1, D_EXPERT, D_MODEL), weight),
            ],
            out_specs=pl.BlockSpec((EXPERT_ROWS, D_MODEL), rows),
        ),
        out_shape=jax.ShapeDtypeStruct(xs.shape, BF16),
        compiler_params=pltpu.CompilerParams(
            dimension_semantics=("arbitrary",), vmem_limit_bytes=VMEM_LIMIT),
        name="moe_experts",
    )(tile_expert, n_tiles, xs, w1, w3, w2)


def _combine_kernel(chunk_expert_ref, pairs_ref, singles_ref, counts_ref,
                    gate_t_ref, earlier_ref, upto_ref, ys_ref, base_ref, o_ref,
                    key_ref, p_ref, rows_ref, sems):
    b = pl.program_id(0)
    last = pl.num_programs(0) - 1
    cur = b % 2
    n_chunks = counts_ref[3 * b]

    def fetch(blk, slot):
        _start_copies(rows_ref.at[slot], ys_ref, pairs_ref, singles_ref, counts_ref, blk,
                      sems.at[slot], False)

    @pl.when(b == 0)
    def _():
        rows_ref[...] = jnp.zeros_like(rows_ref)
        fetch(0, 0)

    @pl.when(b < last)
    def _():
        fetch(jnp.minimum(b + 1, last), 1 - cur)

    group = COMBINE_ROWS
    group_chunks = group // CHUNK
    _slot_keys(gate_t_ref, earlier_ref, upto_ref, key_ref)
    _fill_slot_rows(p_ref, key_ref, chunk_expert_ref, b * BLOCK_CHUNKS, n_chunks, gate_t_ref)

    def clear(c):
        rows = pl.ds(pl.multiple_of((n_chunks + c) * CHUNK, CHUNK), CHUNK)
        p_ref[rows, :] = jnp.zeros((CHUNK, MOE_BLOCK), p_ref.dtype)

    _for_each((group_chunks - n_chunks % group_chunks) % group_chunks, clear)
    _wait_chunks(rows_ref.at[cur], ys_ref, n_chunks, sems.at[cur], False)

    o_ref[...] = base_ref[...]

    def gather(g, carry):
        rows = pl.ds(pl.multiple_of(g * group, group), group)
        o_ref[...] += lax.dot_general(p_ref[rows, :].astype(BF16), rows_ref[cur, rows, :],
                                      (((0,), (0,)), ((), ())), preferred_element_type=F32)
        return carry

    lax.fori_loop(0, (n_chunks * CHUNK + group - 1) // group, gather, 0)


def _combine(chunk_expert, pairs, singles, counts, gate_t, earlier_t, upto, ys, base):
    n = base.shape[0]
    fixed = lambda i, *_: (0, 0)
    row = lambda i, *_: (i, 0)
    return pl.pallas_call(
        _combine_kernel,
        grid_spec=pltpu.PrefetchScalarGridSpec(
            num_scalar_prefetch=4,
            grid=(n // MOE_BLOCK,),
            in_specs=[
                pl.BlockSpec((N_EXPERTS, MOE_BLOCK), lambda i, *_: (0, i)),
                pl.BlockSpec(earlier_t.shape, fixed),
                pl.BlockSpec(upto.shape, fixed),
                pl.BlockSpec(memory_space=pl.ANY),
                pl.BlockSpec((MOE_BLOCK, D_MODEL), row),
            ],
            out_specs=pl.BlockSpec((MOE_BLOCK, D_MODEL), row),
            scratch_shapes=[
                pltpu.VMEM((N_EXPERTS, MOE_BLOCK), F32),
                pltpu.VMEM((BLOCK_SLOTS, MOE_BLOCK), F32),
                pltpu.VMEM((2, BLOCK_SLOTS, D_MODEL), BF16),
                pltpu.SemaphoreType.DMA((2,)),
            ],
        ),
        out_shape=jax.ShapeDtypeStruct((n, D_MODEL), F32),
        compiler_params=pltpu.CompilerParams(
            dimension_semantics=("arbitrary",), vmem_limit_bytes=VMEM_LIMIT),
        name="moe_combine",
    )(chunk_expert, pairs, singles, counts, gate_t, earlier_t, upto, ys, base)


def _moe_layout(counts):
    nb = counts.shape[0]
    chunks = (counts + (CHUNK - 1)) // CHUNK
    first = jnp.cumsum(chunks, axis=1) - chunks
    n_chunks = jnp.sum(chunks, axis=1)
    tile_chunks = EXPERT_ROWS // CHUNK
    expert_tiles = (jnp.sum(chunks, axis=0) + tile_chunks - 1) // tile_chunks
    expert_first = (jnp.cumsum(expert_tiles) - expert_tiles) * tile_chunks
    dst = expert_first[None, :] + jnp.cumsum(chunks, axis=0) - chunks
    c = jnp.arange(BLOCK_CHUNKS)
    last = first + chunks
    chunk_expert = jnp.minimum(jnp.sum(c[None, None, :] >= last[:, :, None], axis=1),
                               N_EXPERTS - 1)

    def copy_list(per_run, offset, width, length):
        ends = jnp.cumsum(per_run, axis=1)
        i = jnp.arange(length)
        run = jnp.minimum(jnp.sum(i[None, None, :] >= ends[:, :, None], axis=1), N_EXPERTS - 1)
        owner = run[:, None, :] == jnp.arange(N_EXPERTS)[None, :, None]
        start = offset - width * (ends - per_run)
        both = (1 << BLOCK_CHUNK_BITS) + 1
        packed = (dst << BLOCK_CHUNK_BITS) + first + both * start
        return (jnp.sum(jnp.where(owner, packed[:, :, None], 0), axis=1)
                + both * width * i[None, :])

    pairs = copy_list(chunks // 2, jnp.zeros_like(chunks), 2, PAIRS_PER_BLOCK)
    singles = copy_list(chunks % 2, chunks - chunks % 2, 1, N_EXPERTS)
    counts3 = jnp.stack([n_chunks, jnp.sum(chunks // 2, axis=1), jnp.sum(chunks % 2, axis=1)],
                        axis=1)
    max_tiles = nb * BLOCK_CHUNKS // tile_chunks + N_EXPERTS
    tile_expert = jnp.minimum(
        jnp.sum(jnp.arange(max_tiles)[:, None] >= jnp.cumsum(expert_tiles)[None, :], axis=1),
        N_EXPERTS - 1)
    expert_chunks = jnp.sum(chunks, axis=0)
    pad = expert_tiles * tile_chunks - expert_chunks
    j = jnp.arange(N_EXPERTS * tile_chunks)
    pad_expert = jnp.minimum(jnp.sum(j[None, :] >= jnp.cumsum(pad)[:, None], axis=0), N_EXPERTS - 1)
    pad_owner = pad_expert[None, :] == jnp.arange(N_EXPERTS)[:, None]
    pad_start = expert_first + expert_chunks - (jnp.cumsum(pad) - pad)
    pad_dst = jnp.sum(jnp.where(pad_owner, pad_start[:, None], 0), axis=0) + j
    i32 = lambda a: a.astype(jnp.int32)
    return (i32(chunk_expert).reshape(-1), i32(pairs).reshape(-1), i32(singles).reshape(-1),
            i32(counts3).reshape(-1), i32(pad_dst), i32(jnp.sum(pad)).reshape(1),
            i32(tile_expert), i32(jnp.sum(expert_tiles)).reshape(1), max_tiles * EXPERT_ROWS)


def _layer(h, g_mix, w_in, lam_re, lam_im, b_re, b_im, c_re, c_im, d_skip, log_step,
           w_glu, b_glu, g_q, g_k, g_out_ssm, g_out_att, w_out, g_ffn, w_router,
           router_bias, e_w1, e_w3, e_w2, s_w1, s_w3, s_w2):
    bsz, seq, d = h.shape
    n = bsz * seq
    x2 = h.reshape(n, d)
    n_heads = D_ATT // HEAD_DIM

    head_id = jnp.arange(D_ATT) // HEAD_DIM
    hsum = jnp.where(head_id[:, None] == head_id[None, :], 1.0 / HEAD_DIM, 0.0).astype(BF16)
    ids = jnp.arange(ATT_TILE)
    later = (ids[:, None] > ids[None, :]).astype(BF16)
    zero, one = jnp.zeros_like(later), jnp.ones_like(later)
    tri = jnp.block([[later, zero, one, zero], [zero, later, zero, one]])

    gq = (jnp.tile(g_q, n_heads) * (math.log2(math.e) / math.sqrt(HEAD_DIM))).reshape(1, D_ATT)
    gk = jnp.tile(g_k, n_heads).reshape(1, D_ATT)
    u, q, k, v = _in_proj(x2, g_mix.reshape(1, d), w_in.astype(BF16), gq, gk, hsum)

    assert bsz == SUBLANES, "the S5 scan keeps one batch entry per sublane"
    s5w = _s5_weights(lam_re, lam_im, b_re, b_im, c_re, c_im, log_step)
    ssm = _s5(u.reshape(bsz, seq, D_SSM), *s5w, d_skip.reshape(1, D_SSM), w_glu.astype(BF16),
              b_glu.reshape(1, D_SSM), g_out_ssm.reshape(1, D_SSM)).reshape(n, D_SSM)

    att = _attention(q.reshape(bsz, seq, D_ATT), k.reshape(bsz, seq, D_ATT),
                     v.reshape(bsz, seq, D_ATT), tri).reshape(n, D_ATT)

    wr_t = w_router.T
    wr_hi = wr_t.astype(BF16)
    wr_lo = (wr_t - wr_hi.astype(F32)).astype(BF16)
    w_out_b = w_out.astype(BF16)
    base, hn, gate_t, counts = _post_mix(
        x2, ssm, att, g_out_att.reshape(1, D_ATT), w_out_b[:D_SSM], w_out_b[D_SSM:],
        g_ffn.reshape(1, d), wr_hi, wr_lo, router_bias.reshape(N_EXPERTS, 1),
        s_w1.astype(BF16), s_w3.astype(BF16), s_w2.astype(BF16))

    w1, w3, w2 = e_w1, e_w3, e_w2
    tok = jnp.arange(MOE_BLOCK)
    earlier_t = (tok[:, None] < tok[None, :]).astype(BF16)
    exp = jnp.arange(N_EXPERTS)
    upto = (exp[None, :] < exp[:, None]).astype(BF16)
    blocks_per_tile = ROW_TILE // MOE_BLOCK
    counts = jnp.round(counts[:, :, :blocks_per_tile]).astype(jnp.int32)
    counts = counts.transpose(0, 2, 1).reshape(n // MOE_BLOCK, N_EXPERTS)
    (chunk_expert, pairs, singles, copy_counts, pad_dst, n_pad, tile_expert, n_tiles,
     total_rows) = _moe_layout(counts)
    xs = _dispatch(chunk_expert, pairs, singles, copy_counts, pad_dst, n_pad, hn, gate_t,
                   earlier_t, upto, total_rows)
    ys = _experts(tile_expert, n_tiles, xs, w1, w3, w2)
    out = _combine(chunk_expert, pairs, singles, copy_counts, gate_t, earlier_t, upto, ys, base)
    return out.reshape(bsz, seq, d)


def kernel(x, g_mix, w_in, lam_re, lam_im, b_re, b_im, c_re, c_im, d_skip, log_step,
           w_glu, b_glu, g_q, g_k, g_out_ssm, g_out_att, w_out, g_ffn, w_router,
           router_bias, e_w1, e_w3, e_w2, s_w1, s_w3, s_w2):
    h = x
    for l in range(g_mix.shape[0]):
        h = _layer(h, g_mix[l], w_in[l], lam_re[l], lam_im[l], b_re[l], b_im[l], c_re[l],
                   c_im[l], d_skip[l], log_step[l], w_glu[l], b_glu[l], g_q[l], g_k[l],
                   g_out_ssm[l], g_out_att[l], w_out[l], g_ffn[l], w_router[l],
                   router_bias[l], e_w1[l], e_w3[l], e_w2[l], s_w1[l], s_w3[l], s_w2[l])
    return h
```

```python
import functools
import math

import jax
import jax.numpy as jnp
from jax import lax
from jax.experimental import pallas as pl
from jax.experimental.pallas import tpu as pltpu

F32 = jnp.float32
BF16 = jnp.bfloat16

D_MODEL = 1024
D_SSM = 512
SSM_GROUP = 16
N_SSM_GROUPS = 32
SSM_STATE = 64
D_ATT = 512
HEAD_DIM = 64
N_EXPERTS = 64
TOP_K = 8
N_EXPERT_GROUPS = 8
TOPK_GROUPS = 4
D_EXPERT = 256
ROUTED_SCALE = 2.5
EPS = 1e-6
LAMBDA_RE_MAX = -1e-4

LANES = 128
SUBLANES = 8
VMEM_LIMIT = 56 * 1024 * 1024

N_STATE = N_SSM_GROUPS * SSM_STATE
GROUPS_PER_TILE = LANES // SSM_GROUP
N_LANE_TILES = D_SSM // LANES
STATE_PER_TILE = GROUPS_PER_TILE * SSM_STATE

ROW_TILE = 1024
S5_CHUNK = 128
S5_COLS = 1024
ATT_TILE = LANES
EXP2_UNDERFLOW = 174.0
ATT_LOOKBACK = 2
ATT_Q_TILES = 2
MOE_BLOCK = 256
CHUNK = 16
BLOCK_CHUNKS = MOE_BLOCK * TOP_K // CHUNK + N_EXPERTS
BLOCK_SLOTS = BLOCK_CHUNKS * CHUNK
PAIRS_PER_BLOCK = BLOCK_CHUNKS // 2
BLOCK_CHUNK_BITS = (BLOCK_CHUNKS - 1).bit_length()
DISPATCH_ROWS = 1536
COMBINE_ROWS = 1536
EXPERT_ROWS = 1024
LOOP_UNROLL = 4

_NT = (((1,), (1,)), ((), ()))


def _rms(x, g):
    ms = jnp.mean(x * x, axis=-1, keepdims=True)
    return x * lax.rsqrt(ms + EPS) * g


def _dot(a, b):
    return jnp.dot(a, b, preferred_element_type=F32)


def _in_proj_kernel(x_ref, g_ref, w_ref, gq_ref, gk_ref, hsum_ref,
                    u_ref, q_ref, k_ref, v_ref):
    hn = _rms(x_ref[...], g_ref[...]).astype(BF16)
    u_ref[...] = _dot(hn, w_ref[:, 0:D_SSM]).astype(BF16)

    def head_norm(lo, g):
        y = _dot(hn, w_ref[:, lo:lo + D_ATT])
        ms = _dot((y * y).astype(BF16), hsum_ref[...])
        return (y * lax.rsqrt(ms + EPS) * g).astype(BF16)

    q_ref[...] = head_norm(D_SSM, gq_ref[...])
    k_ref[...] = head_norm(D_SSM + D_ATT, gk_ref[...])
    v_ref[...] = _dot(hn, w_ref[:, D_SSM + 2 * D_ATT:]).astype(BF16)


def _in_proj(x2, g_mix, w_in, gq, gk, hsum):
    n = x2.shape[0]
    row = lambda i: (i, 0)
    fixed = lambda i: (0, 0)
    out = jax.ShapeDtypeStruct((n, D_SSM), BF16)
    return pl.pallas_call(
        _in_proj_kernel,
        grid=(n // ROW_TILE,),
        in_specs=[
            pl.BlockSpec((ROW_TILE, D_MODEL), row),
            pl.BlockSpec((1, D_MODEL), fixed),
            pl.BlockSpec(w_in.shape, fixed),
            pl.BlockSpec((1, D_ATT), fixed),
            pl.BlockSpec((1, D_ATT), fixed),
            pl.BlockSpec((D_ATT, D_ATT), fixed),
        ],
        out_specs=[pl.BlockSpec((ROW_TILE, D_SSM), row)] * 4,
        out_shape=[out] * 4,
        compiler_params=pltpu.CompilerParams(
            dimension_semantics=("parallel",), vmem_limit_bytes=VMEM_LIMIT),
        name="in_proj",
    )(x2, g_mix, w_in, gq, gk, hsum)


def _gelu_tanh(x):
    c = math.sqrt(2.0 / math.pi)
    return 0.5 * x * (1.0 + jnp.tanh(c * (x + 0.044715 * (x * x * x))))


def _s5_kernel(u_ref, bre_ref, bim_ref, lr_ref, li_ref, cre_ref, cimn_ref,
               d_ref, wg_ref, bg_ref, go_ref, o_ref, sre, sim, hre, him, tb):
    @pl.when(pl.program_id(0) == 0)
    def _():
        hre[...] = jnp.zeros_like(hre)
        him[...] = jnp.zeros_like(him)

    n_batch = u_ref.shape[0]
    tiles = [slice(a * LANES, (a + 1) * LANES) for a in range(N_LANE_TILES)]
    for b in range(n_batch):
        ub = u_ref[b].astype(F32)
        for a, lanes in enumerate(tiles):
            tb[a, pl.ds(b, S5_CHUNK, stride=n_batch), :] = ub[:, lanes]
    u_f32 = jnp.concatenate([tb[a] for a in range(N_LANE_TILES)], axis=1)
    u = u_f32.astype(BF16)
    for a in range(N_LANE_TILES):
        ua = u[:, a * LANES:(a + 1) * LANES]
        cols = slice(a * STATE_PER_TILE, (a + 1) * STATE_PER_TILE)
        sre[:, cols] = _dot(ua, bre_ref[a])
        sim[:, cols] = _dot(ua, bim_ref[a])

    for cb in range(N_STATE // S5_COLS):
        cols = slice(cb * S5_COLS, (cb + 1) * S5_COLS)
        lr = jnp.broadcast_to(lr_ref[:, cols], (SUBLANES, S5_COLS))
        li = jnp.broadcast_to(li_ref[:, cols], (SUBLANES, S5_COLS))

        def step(t, h, cols=cols, lr=lr, li=li):
            hr, hi = h
            rows = pl.ds(pl.multiple_of(t * SUBLANES, SUBLANES), SUBLANES)
            nr = lr * hr - li * hi + sre[rows, cols]
            ni = lr * hi + li * hr + sim[rows, cols]
            sre[rows, cols] = nr
            sim[rows, cols] = ni
            return nr, ni

        hr, hi = lax.fori_loop(0, S5_CHUNK, step, (hre[:, cols], him[:, cols]),
                               unroll=2)
        hre[:, cols] = hr
        him[:, cols] = hi

    ys = []
    for a in range(N_LANE_TILES):
        cols = slice(a * STATE_PER_TILE, (a + 1) * STATE_PER_TILE)
        ys.append(_dot(sre[:, cols].astype(BF16), cre_ref[a])
                  + _dot(sim[:, cols].astype(BF16), cimn_ref[a]))
    y = jnp.concatenate(ys, axis=1)
    y = _gelu_tanh(y + d_ref[...] * u_f32)
    y = y * jax.nn.sigmoid(_dot(y.astype(BF16), wg_ref[...]) + bg_ref[...])
    y = _rms(y, go_ref[...])
    for a, lanes in enumerate(tiles):
        tb[a] = y[:, lanes]
    for b in range(n_batch):
        o_ref[b] = jnp.concatenate(
            [tb[a, pl.ds(b, S5_CHUNK, stride=n_batch), :] for a in range(N_LANE_TILES)],
            axis=1).astype(BF16)


def _s5(u, bre, bim, lr, li, cre, cimn, d_skip, w_glu, b_glu, g_out):
    bsz, seq, _ = u.shape
    rows = S5_CHUNK * bsz
    fixed2 = lambda i: (0, 0)
    fixed3 = lambda i: (0, 0, 0)
    return pl.pallas_call(
        _s5_kernel,
        grid=(seq // S5_CHUNK,),
        in_specs=[
            pl.BlockSpec((bsz, S5_CHUNK, D_SSM), lambda i: (0, i, 0)),
            pl.BlockSpec(bre.shape, fixed3),
            pl.BlockSpec(bim.shape, fixed3),
            pl.BlockSpec((1, N_STATE), fixed2),
            pl.BlockSpec((1, N_STATE), fixed2),
            pl.BlockSpec(cre.shape, fixed3),
            pl.BlockSpec(cimn.shape, fixed3),
            pl.BlockSpec((1, D_SSM), fixed2),
            pl.BlockSpec((D_SSM, D_SSM), fixed2),
            pl.BlockSpec((1, D_SSM), fixed2),
            pl.BlockSpec((1, D_SSM), fixed2),
        ],
        out_specs=pl.BlockSpec((bsz, S5_CHUNK, D_SSM), lambda i: (0, i, 0)),
        out_shape=jax.ShapeDtypeStruct(u.shape, BF16),
        scratch_shapes=[
            pltpu.VMEM((rows, N_STATE), F32),
            pltpu.VMEM((rows, N_STATE), F32),
            pltpu.VMEM((SUBLANES, N_STATE), F32),
            pltpu.VMEM((SUBLANES, N_STATE), F32),
            pltpu.VMEM((N_LANE_TILES, rows, LANES), F32),
        ],
        compiler_params=pltpu.CompilerParams(
            dimension_semantics=("arbitrary",), vmem_limit_bytes=VMEM_LIMIT),
        name="s5_mixer",
    )(u, bre, bim, lr, li, cre, cimn, d_skip, w_glu, b_glu, g_out)


def _s5_weights(lam_re, lam_im, b_re, b_im, c_re, c_im, log_step):
    lre = jnp.minimum(lam_re, LAMBDA_RE_MAX)
    step = jnp.exp(log_step)[:, None]
    mag = jnp.exp(lre * step)
    bar_re = mag * jnp.cos(lam_im * step)
    bar_im = mag * jnp.sin(lam_im * step)
    den = lre * lre + lam_im * lam_im
    coef_re = ((bar_re - 1.0) * lre + bar_im * lam_im) / den
    coef_im = (bar_im * lre - (bar_re - 1.0) * lam_im) / den
    bbar_re = coef_re[..., None] * b_re - coef_im[..., None] * b_im
    bbar_im = coef_re[..., None] * b_im + coef_im[..., None] * b_re
    eye = jnp.eye(GROUPS_PER_TILE, dtype=F32)

    def pack_b(b):
        b = b.reshape(N_LANE_TILES, GROUPS_PER_TILE, SSM_STATE, SSM_GROUP)
        m = jnp.einsum('agph,gk->aghkp', b, eye)
        return m.reshape(N_LANE_TILES, LANES, STATE_PER_TILE).astype(BF16)

    def pack_c(c):
        c = c.reshape(N_LANE_TILES, GROUPS_PER_TILE, SSM_GROUP, SSM_STATE)
        m = jnp.einsum('aghp,gk->agpkh', c, eye)
        return m.reshape(N_LANE_TILES, STATE_PER_TILE, LANES).astype(BF16)

    return (pack_b(bbar_re), pack_b(bbar_im),
            bar_re.reshape(1, N_STATE), bar_im.reshape(1, N_STATE),
            pack_c(c_re), pack_c(-c_im))


def _attn_kernel(q_ref, k_ref, v_ref, tri_ref, o_ref, *scratch):
    acc_refs, car_refs = scratch[:D_ATT // LANES], scratch[D_ATT // LANES:]
    t = ATT_TILE
    first_head = lax.broadcasted_iota(jnp.int32, (1, LANES), 1) < HEAD_DIM
    for s in range(ATT_Q_TILES):
        _attn_tile(pl.program_id(1) * ATT_Q_TILES + s, slice(s * t, (s + 1) * t), first_head,
                   q_ref, k_ref, v_ref, tri_ref, o_ref, acc_refs, car_refs)


def _attn_tile(i, q_rows, first_head, q_ref, k_ref, v_ref, tri_ref, o_ref, acc_refs, car_refs):
    t = ATT_TILE
    scratch = (*acc_refs, *car_refs)

    def split_heads(ref, rows, lanes):
        x = ref[0, rows, lanes]
        zero = jnp.zeros_like(x)
        return jnp.concatenate([jnp.where(first_head, x, zero),
                                jnp.where(first_head, zero, x)], axis=0)

    def visit(first, count, on_diagonal):
        tiles = [slice(p * LANES, (p + 1) * LANES) for p in range(D_ATT // LANES)]
        rows = [pl.ds(pl.multiple_of((first - b) * t, t), t) for b in range(count)]
        if on_diagonal:
            col = lax.broadcasted_iota(jnp.int32, (t, 2 * t), 1)
            row = lax.broadcasted_iota(jnp.int32, (t, 2 * t), 0)
            strictly_before = jnp.where(col >= t, col - t, col) < row
        masked = lambda b: on_diagonal and b == 0
        zs = [[lax.dot_general(q_ref[0, q_rows, lanes], split_heads(k_ref, r, lanes),
                               _NT, preferred_element_type=F32) for lanes in tiles] for r in rows]
        sps = [[jnp.maximum(z, 0.0) + jnp.log2(1.0 + jnp.exp2(-jnp.abs(z))) for z in zb]
               for zb in zs]
        sps = [[jnp.where(strictly_before, sp, 0.0) if masked(b) else sp for sp in spb]
               for b, spb in enumerate(sps)]
        stacked = _dot(jnp.concatenate([sp.astype(BF16) for spb in sps for sp in spb], axis=0),
                       tri_ref[...])
        sums = [[stacked[(b * len(tiles) + p) * t:(b * len(tiles) + p + 1) * t]
                 for p in range(len(tiles))] for b in range(count)]
        smallest = None
        for p, (lanes, acc_ref, car_ref) in enumerate(zip(tiles, acc_refs, car_refs)):
            car = car_ref[...]
            acc = acc_ref[...]
            for b, r in enumerate(rows):
                w = jnp.exp2(zs[b][p] - sps[b][p] - (sums[b][p][:, :2 * t] + car))
                if masked(b):
                    w = jnp.where(strictly_before, w, 0.0)
                acc = acc + _dot(w.astype(BF16), split_heads(v_ref, r, lanes))
                car = car + sums[b][p][:, 2 * t:]
            acc_ref[...] = acc
            car_ref[...] = car
            smallest = car if smallest is None else jnp.minimum(smallest, car)
        return jnp.min(smallest)

    for ref in scratch:
        ref[...] = jnp.zeros_like(ref)
    passes = [functools.partial(visit, i, c + 1, True) for c in range(ATT_LOOKBACK + 1)]
    carried = lax.switch(jnp.minimum(i, ATT_LOOKBACK), passes)

    def more(state):
        j, carried = state
        return (j >= 0) & (carried < EXP2_UNDERFLOW)

    def earlier(state):
        j, _ = state
        return j - 1, visit(j, 1, False)

    lax.while_loop(more, earlier, (i - 1 - ATT_LOOKBACK, carried))
    o_ref[0, q_rows, :] = jnp.concatenate(
        [ref[...] for ref in acc_refs], axis=1).astype(o_ref.dtype)


def _attention(q, k, v, tri):
    bsz, seq, _ = q.shape
    t = ATT_TILE
    n_tiles = D_ATT // LANES
    q_block = ATT_Q_TILES * t
    return pl.pallas_call(
        _attn_kernel,
        grid=(bsz, seq // q_block),
        in_specs=[
            pl.BlockSpec((1, q_block, D_ATT), lambda b, i: (b, i, 0)),
            pl.BlockSpec((1, seq, D_ATT), lambda b, i: (b, 0, 0)),
            pl.BlockSpec((1, seq, D_ATT), lambda b, i: (b, 0, 0)),
            pl.BlockSpec(tri.shape, lambda b, i: (0, 0)),
        ],
        out_specs=pl.BlockSpec((1, q_block, D_ATT), lambda b, i: (b, i, 0)),
        out_shape=jax.ShapeDtypeStruct((bsz, seq, D_ATT), BF16),
        scratch_shapes=([pltpu.VMEM((t, LANES), F32)] * n_tiles
                        + [pltpu.VMEM((t, 2 * t), F32)] * n_tiles),
        compiler_params=pltpu.CompilerParams(
            dimension_semantics=("parallel", "parallel"),
            vmem_limit_bytes=VMEM_LIMIT),
        name="stick_breaking_attention",
    )(q, k, v, tri)


def _pick_first_max(vals, taken, ridx, n):
    cand = jnp.where(taken > 0, -jnp.inf, vals)
    m = jnp.max(cand, axis=0, keepdims=True)
    first = jnp.min(jnp.where((cand == m) & (taken == 0), ridx, n), axis=0, keepdims=True)
    return jnp.where(ridx == first, 1.0, taken)


def _route(scores, biased):
    tm = scores.shape[1]
    per_group = N_EXPERTS // N_EXPERT_GROUPS
    ridx8 = lax.broadcasted_iota(jnp.int32, (per_group, tm), 0)
    zeros8 = jnp.zeros((per_group, tm), F32)
    group_scores = []
    for g in range(N_EXPERT_GROUPS):
        blk = biased[g * per_group:(g + 1) * per_group]
        top1 = _pick_first_max(blk, zeros8, ridx8, per_group)
        top2 = _pick_first_max(blk, top1, ridx8, per_group)
        group_scores.append(jnp.sum(jnp.where(top2 > 0, blk, 0.0), axis=0, keepdims=True))
    grp = jnp.concatenate(group_scores, axis=0)
    ridx_g = lax.broadcasted_iota(jnp.int32, (N_EXPERT_GROUPS, tm), 0)
    grp_taken = jnp.zeros((N_EXPERT_GROUPS, tm), F32)
    for _ in range(TOPK_GROUPS):
        grp_taken = _pick_first_max(grp, grp_taken, ridx_g, N_EXPERT_GROUPS)
    expert_mask = jnp.concatenate(
        [jnp.broadcast_to(grp_taken[g:g + 1], (per_group, tm)) for g in range(N_EXPERT_GROUPS)],
        axis=0)
    sel = jnp.where(expert_mask > 0, biased, -jnp.inf)
    ridx = lax.broadcasted_iota(jnp.int32, (N_EXPERTS, tm), 0)
    taken = jnp.zeros((N_EXPERTS, tm), F32)
    for _ in range(TOP_K):
        taken = _pick_first_max(sel, taken, ridx, N_EXPERTS)
    w = jnp.where(taken > 0, scores, 0.0)
    return w / jnp.sum(w, axis=0, keepdims=True) * ROUTED_SCALE


def _post_mix_kernel(x_ref, ssm_ref, att_ref, ga_ref, wos_ref, woa_ref, gf_ref,
                     wrh_ref, wrl_ref, rb_ref, s1_ref, s3_ref, s2_ref,
                     base_ref, hn_ref, gate_t_ref, cnt_ref):
    att_n = _rms(att_ref[...].astype(F32), ga_ref[...]).astype(BF16)
    h = x_ref[...] + _dot(ssm_ref[...], wos_ref[...]) + _dot(att_n, woa_ref[...])
    hn = _rms(h, gf_ref[...])
    hn_hi = hn.astype(BF16)
    hn_lo = (hn - hn_hi.astype(F32)).astype(BF16)
    hn_ref[...] = hn_hi

    nt = functools.partial(lax.dot_general, dimension_numbers=_NT, preferred_element_type=F32)
    logits = (nt(wrh_ref[...], hn_hi) + nt(wrh_ref[...], hn_lo) + nt(wrl_ref[...], hn_hi))
    scores = jax.nn.sigmoid(logits)
    gate_t = _route(scores, scores + rb_ref[...])
    gate_t_ref[...] = gate_t
    lane = lax.broadcasted_iota(jnp.int32, (1, LANES), 1)
    routed = jnp.where(gate_t > 0.0, 1.0, 0.0)
    counts = jnp.zeros((N_EXPERTS, LANES), F32)
    for blk in range(ROW_TILE // MOE_BLOCK):
        c = jnp.sum(routed[:, blk * MOE_BLOCK:(blk + 1) * MOE_BLOCK], axis=1, keepdims=True)
        counts = jnp.where(lane == blk, c, counts)
    cnt_ref[0] = counts

    a = _dot(hn_hi, s1_ref[...])
    b = _dot(hn_hi, s3_ref[...])
    act = (a * jax.nn.sigmoid(a) * b).astype(BF16)
    base_ref[...] = h + _dot(act, s2_ref[...])


def _post_mix(x2, ssm, att, g_att, wo_s, wo_a, g_ffn, wr_hi, wr_lo, rbias, s1, s3, s2):
    n = x2.shape[0]
    row = lambda i: (i, 0)
    fixed = lambda i: (0, 0)
    full = lambda a: pl.BlockSpec(a.shape, fixed)
    return pl.pallas_call(
        _post_mix_kernel,
        grid=(n // ROW_TILE,),
        in_specs=[
            pl.BlockSpec((ROW_TILE, D_MODEL), row),
            pl.BlockSpec((ROW_TILE, D_SSM), row),
            pl.BlockSpec((ROW_TILE, D_ATT), row),
            full(g_att), full(wo_s), full(wo_a), full(g_ffn),
            full(wr_hi), full(wr_lo), full(rbias), full(s1), full(s3), full(s2),
        ],
        out_specs=[
            pl.BlockSpec((ROW_TILE, D_MODEL), row),
            pl.BlockSpec((ROW_TILE, D_MODEL), row),
            pl.BlockSpec((N_EXPERTS, ROW_TILE), lambda i: (0, i)),
            pl.BlockSpec((1, N_EXPERTS, LANES), lambda i: (i, 0, 0)),
        ],
        out_shape=[
            jax.ShapeDtypeStruct((n, D_MODEL), F32),
            jax.ShapeDtypeStruct((n, D_MODEL), BF16),
            jax.ShapeDtypeStruct((N_EXPERTS, n), F32),
            jax.ShapeDtypeStruct((n // ROW_TILE, N_EXPERTS, LANES), F32),
        ],
        compiler_params=pltpu.CompilerParams(
            dimension_semantics=("parallel",), vmem_limit_bytes=VMEM_LIMIT),
        name="post_mix",
    )(x2, ssm, att, g_att, wo_s, wo_a, g_ffn, wr_hi, wr_lo, rbias, s1, s3, s2)


def _slot_keys(gate_t_ref, earlier_ref, upto_ref, key_ref):
    sel = gate_t_ref[...] > 0.0
    sel_f = jnp.where(sel, 1.0, 0.0)
    rank = _dot(sel_f.astype(BF16), earlier_ref[...])
    count = jnp.sum(sel_f, axis=1, keepdims=True)
    padded = jnp.floor((count + (CHUNK - 1)) * (1.0 / CHUNK)) * CHUNK
    first = _dot(upto_ref[...], jnp.broadcast_to(padded, (N_EXPERTS, LANES)).astype(BF16))
    first = jnp.concatenate([first] * (MOE_BLOCK // LANES), axis=1)
    key_ref[...] = jnp.where(sel, rank + first, -1.0)


def _for_each(n, body):
    trips = n // LOOP_UNROLL

    def unrolled(i, carry):
        for u in range(LOOP_UNROLL):
            body(i * LOOP_UNROLL + u)
        return carry

    def single(i, carry):
        body(i)
        return carry

    lax.fori_loop(0, trips, unrolled, 0)
    lax.fori_loop(trips * LOOP_UNROLL, n, single, 0)


def _fill_slot_rows(p_ref, key_ref, chunk_expert_ref, base, n_chunks, value_ref=None):
    within = lax.broadcasted_iota(jnp.int32, (CHUNK, MOE_BLOCK), 0).astype(F32)

    def fill(c):
        e = chunk_expert_ref[base + c]
        hit = (key_ref[pl.ds(e, 1), :] - (c * CHUNK).astype(F32)) == within
        vals = 1.0 if value_ref is None else value_ref[pl.ds(e, 1), :]
        rows = pl.ds(pl.multiple_of(c * CHUNK, CHUNK), CHUNK)
        p_ref[rows, :] = jnp.where(hit, vals, 0.0).astype(p_ref.dtype)

    _for_each(n_chunks, fill)


def _rows_copy(block_rows_ref, global_ref, c, dst, n, sem, to_global):
    local = block_rows_ref.reshape(BLOCK_CHUNKS, CHUNK, D_MODEL).at[pl.ds(c, n)]
    remote = global_ref.reshape(global_ref.shape[0] // CHUNK, CHUNK, D_MODEL).at[pl.ds(dst, n)]
    return (pltpu.make_async_copy(local, remote, sem) if to_global
            else pltpu.make_async_copy(remote, local, sem))


def _wait_chunks(block_rows_ref, global_ref, n_chunks, sem, to_global):
    bit = 1 << (BLOCK_CHUNKS.bit_length() - 1)
    while bit:
        @pl.when((n_chunks & bit) != 0)
        def _(bit=bit):
            _rows_copy(block_rows_ref, global_ref, 0, 0, bit, sem, to_global).wait()
        bit >>= 1


def _start_copies(block_rows_ref, global_ref, pairs_ref, singles_ref, counts_ref, blk, sem,
                  to_global):
    for width, list_ref, per_block, which in ((2, pairs_ref, PAIRS_PER_BLOCK, 1),
                                              (1, singles_ref, N_EXPERTS, 2)):
        def start(i, width=width, list_ref=list_ref, per_block=per_block):
            packed = list_ref[blk * per_block + i]
            _rows_copy(block_rows_ref, global_ref, packed & (1 << BLOCK_CHUNK_BITS) - 1,
                       packed >> BLOCK_CHUNK_BITS, width, sem, to_global).start()

        _for_each(counts_ref[3 * blk + which], start)


def _dispatch_kernel(chunk_expert_ref, pairs_ref, singles_ref, counts_ref, pad_dst_ref, n_pad_ref,
                     hn_ref, gate_t_ref, earlier_ref, upto_ref, xs_ref,
                     key_ref, p_ref, rows_ref, zero_ref, sems, zero_sem):
    b = pl.program_id(0)
    last = pl.num_programs(0) - 1
    cur = b % 2
    base = b * BLOCK_CHUNKS
    n_chunks = counts_ref[3 * b]

    @pl.when(b == 0)
    def _():
        p_ref[...] = jnp.zeros_like(p_ref)
        zero_ref[...] = jnp.zeros_like(zero_ref)

        def pad_copy(j):
            dst = pl.ds(pl.multiple_of(pad_dst_ref[j] * CHUNK, CHUNK), CHUNK)
            return pltpu.make_async_copy(zero_ref, xs_ref.at[dst, :], zero_sem)

        _for_each(n_pad_ref[0], lambda j: pad_copy(j).start())
        _for_each(n_pad_ref[0], lambda j: pad_copy(j).wait())

    _slot_keys(gate_t_ref, earlier_ref, upto_ref, key_ref)
    _fill_slot_rows(p_ref, key_ref, chunk_expert_ref, base, n_chunks)

    group = DISPATCH_ROWS

    def permute(g, carry):
        rows = pl.ds(pl.multiple_of(g * group, group), group)
        rows_ref[cur, rows, :] = _dot(p_ref[rows, :], hn_ref[...]).astype(BF16)
        return carry

    lax.fori_loop(0, (n_chunks * CHUNK + group - 1) // group, permute, 0)

    _start_copies(rows_ref.at[cur], xs_ref, pairs_ref, singles_ref, counts_ref, b, sems.at[cur],
                  True)

    @pl.when(b > 0)
    def _():
        _wait_chunks(rows_ref.at[1 - cur], xs_ref, counts_ref[3 * jnp.maximum(b - 1, 0)],
                     sems.at[1 - cur], True)

    @pl.when(b == last)
    def _():
        _wait_chunks(rows_ref.at[cur], xs_ref, n_chunks, sems.at[cur], True)


def _dispatch(chunk_expert, pairs, singles, counts, pad_dst, n_pad, hn, gate_t, earlier_t, upto,
              total_rows):
    nb = hn.shape[0] // MOE_BLOCK
    fixed = lambda i, *_: (0, 0)
    return pl.pallas_call(
        _dispatch_kernel,
        grid_spec=pltpu.PrefetchScalarGridSpec(
            num_scalar_prefetch=6,
            grid=(nb,),
            in_specs=[
                pl.BlockSpec((MOE_BLOCK, D_MODEL), lambda i, *_: (i, 0)),
                pl.BlockSpec((N_EXPERTS, MOE_BLOCK), lambda i, *_: (0, i)),
                pl.BlockSpec(earlier_t.shape, fixed),
                pl.BlockSpec(upto.shape, fixed),
            ],
            out_specs=pl.BlockSpec(memory_space=pl.ANY),
            scratch_shapes=[
                pltpu.VMEM((N_EXPERTS, MOE_BLOCK), F32),
                pltpu.VMEM((BLOCK_SLOTS, MOE_BLOCK), BF16),
                pltpu.VMEM((2, BLOCK_SLOTS, D_MODEL), BF16),
                pltpu.VMEM((CHUNK, D_MODEL), BF16),
                pltpu.SemaphoreType.DMA((2,)),
                pltpu.SemaphoreType.DMA(()),
            ],
        ),
        out_shape=jax.ShapeDtypeStruct((total_rows, D_MODEL), BF16),
        compiler_params=pltpu.CompilerParams(
            dimension_semantics=("arbitrary",), vmem_limit_bytes=VMEM_LIMIT),
        name="moe_dispatch",
    )(chunk_expert, pairs, singles, counts, pad_dst, n_pad, hn, gate_t, earlier_t, upto)


def _expert_kernel(tile_expert_ref, n_tiles_ref, x_ref, w1_ref, w3_ref, w2_ref, y_ref):
    @pl.when(pl.program_id(0) < n_tiles_ref[0])
    def _():
        x = x_ref[...]
        a = _dot(x, w1_ref[0].astype(BF16))
        b = _dot(x, w3_ref[0].astype(BF16))
        act = (a * jax.nn.sigmoid(a) * b).astype(BF16)
        y_ref[...] = _dot(act, w2_ref[0].astype(BF16)).astype(BF16)


def _experts(tile_expert, n_tiles, xs, w1, w3, w2):
    rows = lambda m, te, nt: (jnp.minimum(m, nt[0] - 1), 0)
    weight = lambda m, te, nt: (te[m], 0, 0)
    return pl.pallas_call(
        _expert_kernel,
        grid_spec=pltpu.PrefetchScalarGridSpec(
            num_scalar_prefetch=2,
            grid=(xs.shape[0] // EXPERT_ROWS,),
            in_specs=[
                pl.BlockSpec((EXPERT_ROWS, D_MODEL), rows),
                pl.BlockSpec((1, D_MODEL, D_EXPERT), weight),
                pl.BlockSpec((1, D_MODEL, D_EXPERT), weight),
                pl.BlockSpec((1, D_EXPERT, D_MODEL), weight),
            ],
            out_specs=pl.BlockSpec((EXPERT_ROWS, D_MODEL), rows),
        ),
        out_shape=jax.ShapeDtypeStruct(xs.shape, BF16),
        compiler_params=pltpu.CompilerParams(
            dimension_semantics=("arbitrary",), vmem_limit_bytes=VMEM_LIMIT),
        name="moe_experts",
    )(tile_expert, n_tiles, xs, w1, w3, w2)


def _combine_kernel(chunk_expert_ref, pairs_ref, singles_ref, counts_ref,
                    gate_t_ref, earlier_ref, upto_ref, ys_ref, base_ref, o_ref,
                    key_ref, p_ref, rows_ref, sems):
    b = pl.program_id(0)
    last = pl.num_programs(0) - 1
    cur = b % 2
    n_chunks = counts_ref[3 * b]

    def fetch(blk, slot):
        _start_copies(rows_ref.at[slot], ys_ref, pairs_ref, singles_ref, counts_ref, blk,
                      sems.at[slot], False)

    @pl.when(b == 0)
    def _():
        rows_ref[...] = jnp.zeros_like(rows_ref)
        fetch(0, 0)

    @pl.when(b < last)
    def _():
        fetch(jnp.minimum(b + 1, last), 1 - cur)

    group = COMBINE_ROWS
    group_chunks = group // CHUNK
    _slot_keys(gate_t_ref, earlier_ref, upto_ref, key_ref)
    _fill_slot_rows(p_ref, key_ref, chunk_expert_ref, b * BLOCK_CHUNKS, n_chunks, gate_t_ref)

    def clear(c):
        rows = pl.ds(pl.multiple_of((n_chunks + c) * CHUNK, CHUNK), CHUNK)
        p_ref[rows, :] = jnp.zeros((CHUNK, MOE_BLOCK), p_ref.dtype)

    _for_each((group_chunks - n_chunks % group_chunks) % group_chunks, clear)
    _wait_chunks(rows_ref.at[cur], ys_ref, n_chunks, sems.at[cur], False)

    o_ref[...] = base_ref[...]

    def gather(g, carry):
        rows = pl.ds(pl.multiple_of(g * group, group), group)
        o_ref[...] += lax.dot_general(p_ref[rows, :].astype(BF16), rows_ref[cur, rows, :],
                                      (((0,), (0,)), ((), ())), preferred_element_type=F32)
        return carry

    lax.fori_loop(0, (n_chunks * CHUNK + group - 1) // group, gather, 0)


def _combine(chunk_expert, pairs, singles, counts, gate_t, earlier_t, upto, ys, base):
    n = base.shape[0]
    fixed = lambda i, *_: (0, 0)
    row = lambda i, *_: (i, 0)
    return pl.pallas_call(
        _combine_kernel,
        grid_spec=pltpu.PrefetchScalarGridSpec(
            num_scalar_prefetch=4,
            grid=(n // MOE_BLOCK,),
            in_specs=[
                pl.BlockSpec((N_EXPERTS, MOE_BLOCK), lambda i, *_: (0, i)),
                pl.BlockSpec(earlier_t.shape, fixed),
                pl.BlockSpec(upto.shape, fixed),
                pl.BlockSpec(memory_space=pl.ANY),
                pl.BlockSpec((MOE_BLOCK, D_MODEL), row),
            ],
            out_specs=pl.BlockSpec((MOE_BLOCK, D_MODEL), row),
            scratch_shapes=[
                pltpu.VMEM((N_EXPERTS, MOE_BLOCK), F32),
                pltpu.VMEM((BLOCK_SLOTS, MOE_BLOCK), F32),
                pltpu.VMEM((2, BLOCK_SLOTS, D_MODEL), BF16),
                pltpu.SemaphoreType.DMA((2,)),
            ],
        ),
        out_shape=jax.ShapeDtypeStruct((n, D_MODEL), F32),
        compiler_params=pltpu.CompilerParams(
            dimension_semantics=("arbitrary",), vmem_limit_bytes=VMEM_LIMIT),
        name="moe_combine",
    )(chunk_expert, pairs, singles, counts, gate_t, earlier_t, upto, ys, base)


def _moe_layout(counts):
    nb = counts.shape[0]
    chunks = (counts + (CHUNK - 1)) // CHUNK
    first = jnp.cumsum(chunks, axis=1) - chunks
    n_chunks = jnp.sum(chunks, axis=1)
    tile_chunks = EXPERT_ROWS // CHUNK
    expert_tiles = (jnp.sum(chunks, axis=0) + tile_chunks - 1) // tile_chunks
    expert_first = (jnp.cumsum(expert_tiles) - expert_tiles) * tile_chunks
    dst = expert_first[None, :] + jnp.cumsum(chunks, axis=0) - chunks
    c = jnp.arange(BLOCK_CHUNKS)
    last = first + chunks
    chunk_expert = jnp.minimum(jnp.sum(c[None, None, :] >= last[:, :, None], axis=1),
                               N_EXPERTS - 1)

    def copy_list(per_run, offset, width, length):
        ends = jnp.cumsum(per_run, axis=1)
        i = jnp.arange(length)
        run = jnp.minimum(jnp.sum(i[None, None, :] >= ends[:, :, None], axis=1), N_EXPERTS - 1)
        owner = run[:, None, :] == jnp.arange(N_EXPERTS)[None, :, None]
        start = offset - width * (ends - per_run)
        both = (1 << BLOCK_CHUNK_BITS) + 1
        packed = (dst << BLOCK_CHUNK_BITS) + first + both * start
        return (jnp.sum(jnp.where(owner, packed[:, :, None], 0), axis=1)
                + both * width * i[None, :])

    pairs = copy_list(chunks // 2, jnp.zeros_like(chunks), 2, PAIRS_PER_BLOCK)
    singles = copy_list(chunks % 2, chunks - chunks % 2, 1, N_EXPERTS)
    counts3 = jnp.stack([n_chunks, jnp.sum(chunks // 2, axis=1), jnp.sum(chunks % 2, axis=1)],
                        axis=1)
    max_tiles = nb * BLOCK_CHUNKS // tile_chunks + N_EXPERTS
    tile_expert = jnp.minimum(
        jnp.sum(jnp.arange(max_tiles)[:, None] >= jnp.cumsum(expert_tiles)[None, :], axis=1),
        N_EXPERTS - 1)
    expert_chunks = jnp.sum(chunks, axis=0)
    pad = expert_tiles * tile_chunks - expert_chunks
    j = jnp.arange(N_EXPERTS * tile_chunks)
    pad_expert = jnp.minimum(jnp.sum(j[None, :] >= jnp.cumsum(pad)[:, None], axis=0), N_EXPERTS - 1)
    pad_owner = pad_expert[None, :] == jnp.arange(N_EXPERTS)[:, None]
    pad_start = expert_first + expert_chunks - (jnp.cumsum(pad) - pad)
    pad_dst = jnp.sum(jnp.where(pad_owner, pad_start[:, None], 0), axis=0) + j
    i32 = lambda a: a.astype(jnp.int32)
    return (i32(chunk_expert).reshape(-1), i32(pairs).reshape(-1), i32(singles).reshape(-1),
            i32(counts3).reshape(-1), i32(pad_dst), i32(jnp.sum(pad)).reshape(1),
            i32(tile_expert), i32(jnp.sum(expert_tiles)).reshape(1), max_tiles * EXPERT_ROWS)


def _layer(h, g_mix, w_in, lam_re, lam_im, b_re, b_im, c_re, c_im, d_skip, log_step,
           w_glu, b_glu, g_q, g_k, g_out_ssm, g_out_att, w_out, g_ffn, w_router,
           router_bias, e_w1, e_w3, e_w2, s_w1, s_w3, s_w2):
    bsz, seq, d = h.shape
    n = bsz * seq
    x2 = h.reshape(n, d)
    n_heads = D_ATT // HEAD_DIM

    head_id = jnp.arange(D_ATT) // HEAD_DIM
    hsum = jnp.where(head_id[:, None] == head_id[None, :], 1.0 / HEAD_DIM, 0.0).astype(BF16)
    ids = jnp.arange(ATT_TILE)
    later = (ids[:, None] > ids[None, :]).astype(BF16)
    zero, one = jnp.zeros_like(later), jnp.ones_like(later)
    tri = jnp.block([[later, zero, one, zero], [zero, later, zero, one]])

    gq = (jnp.tile(g_q, n_heads) * (math.log2(math.e) / math.sqrt(HEAD_DIM))).reshape(1, D_ATT)
    gk = jnp.tile(g_k, n_heads).reshape(1, D_ATT)
    u, q, k, v = _in_proj(x2, g_mix.reshape(1, d), w_in.astype(BF16), gq, gk, hsum)

    assert bsz == SUBLANES, "the S5 scan keeps one batch entry per sublane"
    s5w = _s5_weights(lam_re, lam_im, b_re, b_im, c_re, c_im, log_step)
    ssm = _s5(u.reshape(bsz, seq, D_SSM), *s5w, d_skip.reshape(1, D_SSM), w_glu.astype(BF16),
              b_glu.reshape(1, D_SSM), g_out_ssm.reshape(1, D_SSM)).reshape(n, D_SSM)

    att = _attention(q.reshape(bsz, seq, D_ATT), k.reshape(bsz, seq, D_ATT),
                     v.reshape(bsz, seq, D_ATT), tri).reshape(n, D_ATT)

    wr_t = w_router.T
    wr_hi = wr_t.astype(BF16)
    wr_lo = (wr_t - wr_hi.astype(F32)).astype(BF16)
    w_out_b = w_out.astype(BF16)
    base, hn, gate_t, counts = _post_mix(
        x2, ssm, att, g_out_att.reshape(1, D_ATT), w_out_b[:D_SSM], w_out_b[D_SSM:],
        g_ffn.reshape(1, d), wr_hi, wr_lo, router_bias.reshape(N_EXPERTS, 1),
        s_w1.astype(BF16), s_w3.astype(BF16), s_w2.astype(BF16))

    w1, w3, w2 = e_w1, e_w3, e_w2
    tok = jnp.arange(MOE_BLOCK)
    earlier_t = (tok[:, None] < tok[None, :]).astype(BF16)
    exp = jnp.arange(N_EXPERTS)
    upto = (exp[None, :] < exp[:, None]).astype(BF16)
    blocks_per_tile = ROW_TILE // MOE_BLOCK
    counts = jnp.round(counts[:, :, :blocks_per_tile]).astype(jnp.int32)
    counts = counts.transpose(0, 2, 1).reshape(n // MOE_BLOCK, N_EXPERTS)
    (chunk_expert, pairs, singles, copy_counts, pad_dst, n_pad, tile_expert, n_tiles,
     total_rows) = _moe_layout(counts)
    xs = _dispatch(chunk_expert, pairs, singles, copy_counts, pad_dst, n_pad, hn, gate_t,
                   earlier_t, upto, total_rows)
    ys = _experts(tile_expert, n_tiles, xs, w1, w3, w2)
    out = _combine(chunk_expert, pairs, singles, copy_counts, gate_t, earlier_t, upto, ys, base)
    return out.reshape(bsz, seq, d)


def kernel(x, g_mix, w_in, lam_re, lam_im, b_re, b_im, c_re, c_im, d_skip, log_step,
           w_glu, b_glu, g_q, g_k, g_out_ssm, g_out_att, w_out, g_ffn, w_router,
           router_bias, e_w1, e_w3, e_w2, s_w1, s_w3, s_w2):
    h = x
    for l in range(g_mix.shape[0]):
        h = _layer(h, g_mix[l], w_in[l], lam_re[l], lam_im[l], b_re[l], b_im[l], c_re[l],
                   c_im[l], d_skip[l], log_step[l], w_glu[l], b_glu[l], g_q[l], g_k[l],
                   g_out_ssm[l], g_out_att[l], w_out[l], g_ffn[l], w_router[l],
                   router_bias[l], e_w1[l], e_w3[l], e_w2[l], s_w1[l], s_w3[l], s_w2[l])
    return h
```
